```python
import math
import jax, jax.numpy as jnp
from jax import lax
import numpy as np

D_MODEL = 1024
BATCH = 4
SEQ = 8192
DEPTH = 2

CHUNK = 64
QBLK = 128
MEM_LEN = 256
EPS = 1e-6

POOL_WIDTH = D_MODEL // 4
POOL_GROUPS = 4
POOL_GDIM = POOL_WIDTH // POOL_GROUPS
POOL_WINDOWS = (2, 4, 8, 16)

RWKV_WIDTH = D_MODEL // 4
RWKV_HEAD = 64
RWKV_HEADS = RWKV_WIDTH // RWKV_HEAD
D_DECAY_LORA = 64
D_AAA_LORA = 64
D_GATE_LORA = 128
RWKV_COLS = 3 * RWKV_WIDTH + D_DECAY_LORA + D_AAA_LORA + D_GATE_LORA
RWKV_SPLITS = tuple(int(i) for i in np.cumsum(
    [RWKV_WIDTH, RWKV_WIDTH, RWKV_WIDTH, D_DECAY_LORA, D_AAA_LORA]))
RWKV_GN_EPS = 64e-5

DIFF_WIDTH = D_MODEL // 2
DIFF_HEADS = 4
DIFF_VDIM = DIFF_WIDTH // DIFF_HEADS
DIFF_QK = DIFF_VDIM // 2
DIFF_COLS = 3 * DIFF_WIDTH

P_IN = POOL_WIDTH + RWKV_COLS + DIFF_COLS

XA_HEADS = 4
XA_HEAD = D_MODEL // XA_HEADS
D_FF = 2816
CONV_W = 3

kernel_name = "hybrid_pool_rwkv7_diffattn_streaming_encoder"

F32 = jnp.float32


def rms_normalize(x, eps=EPS):
    xf = x.astype(F32)
    return xf * lax.rsqrt(jnp.mean(xf * xf, axis=-1, keepdims=True) + eps)


def rmsnorm(x, g):
    return (rms_normalize(x) * g.astype(F32)).astype(x.dtype)


def shift_right(x, n=1):
    return jnp.pad(x, ((0, 0), (n, 0), (0, 0)))[:, :x.shape[1]]


def alibi_slopes(n_heads):
    return jnp.asarray(2.0 ** (-8.0 * (np.arange(n_heads) + 1) / n_heads), dtype=F32)


def pool_mixer(u, w_grp, scale):
    B, S, _ = u.shape
    ug = u.astype(F32).reshape(B, S, POOL_GROUPS, POOL_GDIM)
    t = jnp.arange(S)
    outs = []
    for gi, w in enumerate(POOL_WINDOWS):
        c = jnp.cumsum(ug[:, :, gi], axis=1)
        win_sum = c - shift_right(c, w)
        count = jnp.minimum(t + 1, w).astype(F32)[None, :, None]
        outs.append(win_sum / count - ug[:, :, gi])
    d = jnp.stack(outs, axis=2)
    y = jnp.einsum('bsgc,gcd->bsgd', d, w_grp.astype(F32))
    return (y.reshape(B, S, POOL_WIDTH) * scale.astype(F32)).astype(u.dtype)


def wkv7_scan(r, decay, k, v, a_vec, b_vec):
    B, S, H, N = r.shape

    def step(s, inp):
        r_t, w_t, k_t, v_t, a_t, b_t = inp
        sa = jnp.einsum('bhvk,bhk->bhv', s, a_t)
        s = (s * w_t[:, :, None, :] + sa[..., None] * b_t[:, :, None, :]
             + v_t[..., None] * k_t[:, :, None, :])
        y = jnp.einsum('bhvk,bhk->bhv', s, r_t)
        return s, y

    seq_first = lambda a: jnp.moveaxis(a, 1, 0)
    s0 = jnp.zeros((B, H, N, N), F32)
    _, ys = lax.scan(step, s0, (seq_first(r), seq_first(decay), seq_first(k),
                                seq_first(v), seq_first(a_vec), seq_first(b_vec)))
    return jnp.moveaxis(ys, 0, 1)


def rwkv7_mixer(z, mu, w0, w2, a0, a2, g2, k_k, k_a, r_k, ln_w, ln_b):
    B, S, _ = z.shape
    H, N = RWKV_HEADS, RWKV_HEAD
    zf = z.astype(F32)
    zm = zf + mu.astype(F32) * (shift_right(zf) - zf)
    r, k, v, wd, ad, gd = jnp.split(zm, RWKV_SPLITS, axis=-1)
    w = -jax.nn.softplus(-(w0 + jnp.tanh(wd) @ w2)) - 0.5
    a = jax.nn.sigmoid(a0 + ad @ a2)
    g = jax.nn.sigmoid(gd) @ g2
    heads = lambda t: t.reshape(B, S, H, N)
    r, w, k, v, a = heads(r), heads(w), heads(k), heads(v), heads(a)
    kk = k * k_k.astype(F32).reshape(H, N)
    kk = kk / jnp.maximum(jnp.sqrt(jnp.sum(kk * kk, -1, keepdims=True)), 1e-12)
    k = k * (1.0 + (a - 1.0) * k_a.astype(F32).reshape(H, N))
    decay = jnp.exp(-jnp.exp(w))
    y = wkv7_scan(r, decay, k, v, -kk, kk * a)
    mean = jnp.mean(y, -1, keepdims=True)
    var = jnp.mean(jnp.square(y - mean), -1, keepdims=True)
    y = ((y - mean) * lax.rsqrt(var + RWKV_GN_EPS)).reshape(B, S, RWKV_WIDTH)
    y = y * ln_w.astype(F32) + ln_b.astype(F32)
    bonus = jnp.sum(r * k * r_k.astype(F32), -1, keepdims=True) * v
    y = y + bonus.reshape(B, S, RWKV_WIDTH)
    return (y * g).astype(z.dtype)


def diff_attention(z, q_norm, k_norm, lq1, lk1, lq2, lk2, subln, lambda_init):
    B, S, _ = z.shape
    H = DIFF_HEADS
    q, k, v = jnp.split(z.astype(F32), 3, axis=-1)
    q = rms_normalize(q.reshape(B, S, H, 2, DIFF_QK)) * q_norm.astype(F32)
    k = rms_normalize(k.reshape(B, S, H, 2, DIFF_QK)) * k_norm.astype(F32)
    v = v.reshape(B, S, H, DIFF_VDIM)
    lam = (jnp.exp(jnp.sum(lq1.astype(F32) * lk1.astype(F32)))
           - jnp.exp(jnp.sum(lq2.astype(F32) * lk2.astype(F32))) + lambda_init)
    slopes = alibi_slopes(H)
    nblk = S // QBLK
    qb = q.reshape(B, nblk, QBLK, H, 2, DIFF_QK).transpose(1, 0, 2, 3, 4, 5)
    kpos = jnp.arange(S)
    kchunk = kpos // CHUNK
    scale = DIFF_QK ** -0.5

    def block(args):
        q_blk, start = args
        qpos = start + jnp.arange(QBLK)
        s = jnp.einsum('bqhcd,bkhcd->bhcqk', q_blk, k) * scale
        dist = jnp.abs(qpos[:, None] - kpos[None, :]).astype(F32)
        bias = -slopes[:, None, None, None] * dist
        visible = kchunk[None, :] <= (qpos // CHUNK)[:, None]
        s = jnp.where(visible, s + bias, -jnp.inf)
        p = jax.nn.softmax(s, axis=-1)
        p = p[:, :, 0] - lam * p[:, :, 1]
        return jnp.einsum('bhqk,bkhd->bqhd', p, v)

    starts = jnp.arange(nblk) * QBLK
    o = lax.map(block, (qb, starts))
    o = jnp.moveaxis(o, 0, 1).reshape(B, S, H, DIFF_VDIM)
    o = rms_normalize(o) * subln.astype(F32) * (1.0 - lambda_init)
    return o.reshape(B, S, DIFF_WIDTH).astype(z.dtype)


def memory_cross_attention(xn, memn, wq, wk, wv, wo, q_norm, k_norm):
    B, S, _ = xn.shape
    M = memn.shape[1]
    q = rms_normalize((xn @ wq).reshape(B, S, XA_HEADS, XA_HEAD)) * q_norm.astype(F32)
    k = rms_normalize((memn @ wk).reshape(B, M, XA_HEADS, XA_HEAD)) * k_norm.astype(F32)
    v = (memn @ wv).reshape(B, M, XA_HEADS, XA_HEAD).astype(F32)
    s = jnp.einsum('bshd,bmhd->bhsm', q, k) * (XA_HEAD ** -0.5)
    p = jax.nn.softmax(s, axis=-1)
    o = jnp.einsum('bhsm,bmhd->bshd', p, v).reshape(B, S, D_MODEL).astype(xn.dtype)
    return o @ wo


def conv_ffn(xn, w_up, conv_w, conv_b, w_down):
    h = xn @ w_up
    a, b = jnp.split(h, 2, axis=-1)
    rhs = conv_w.reshape(CONV_W, 1, D_FF).astype(a.dtype)
    c = lax.conv_general_dilated(a, rhs, window_strides=(1,), padding=[(CONV_W - 1, 0)],
                                 dimension_numbers=('NWC', 'WIO', 'NWC'),
                                 feature_group_count=D_FF) + conv_b
    return (jax.nn.gelu(c, approximate=False) * b) @ w_down


def setup_inputs(seed: int = 0) -> dict:
    key = jax.random.key(seed)
    ks = iter(jax.random.split(key, 40))
    nrm = lambda shape, s: jax.random.normal(next(ks), shape, F32) * s
    gain = lambda shape: 1.0 + 0.02 * jax.random.normal(next(ks), shape, F32)
    L = DEPTH
    return {
        "x": nrm((BATCH, SEQ, D_MODEL), 1.0),
        "mem": nrm((BATCH, MEM_LEN, D_MODEL), 1.0),
        "mix_norm_g": gain((L, D_MODEL)),
        "w_in": nrm((L, D_MODEL, P_IN), D_MODEL ** -0.5),
        "pool_w": nrm((L, POOL_GROUPS, POOL_GDIM, POOL_GDIM), POOL_GDIM ** -0.5),
        "pool_scale": gain((L, POOL_WIDTH)),
        "rwkv_mu": jax.random.uniform(next(ks), (L, RWKV_COLS), F32),
        "rwkv_w0": jax.random.uniform(next(ks), (L, RWKV_WIDTH), F32, -5.0, -1.0),
        "rwkv_w2": nrm((L, D_DECAY_LORA, RWKV_WIDTH), 0.5 * D_DECAY_LORA ** -0.5),
        "rwkv_a0": nrm((L, RWKV_WIDTH), 0.1),
        "rwkv_a2": nrm((L, D_AAA_LORA, RWKV_WIDTH), 0.5 * D_AAA_LORA ** -0.5),
        "rwkv_g2": nrm((L, D_GATE_LORA, RWKV_WIDTH), D_GATE_LORA ** -0.5),
        "rwkv_k_k": 0.85 + nrm((L, RWKV_WIDTH), 0.05),
        "rwkv_k_a": 1.0 + nrm((L, RWKV_WIDTH), 0.05),
        "rwkv_r_k": nrm((L, RWKV_HEADS, RWKV_HEAD), 0.1),
        "rwkv_ln_w": gain((L, RWKV_WIDTH)),
        "rwkv_ln_b": nrm((L, RWKV_WIDTH), 0.02),
        "diff_q_norm": gain((L, 2, DIFF_QK)),
        "diff_k_norm": gain((L, 2, DIFF_QK)),
        "diff_lq1": nrm((L, DIFF_QK), 0.1),
        "diff_lk1": nrm((L, DIFF_QK), 0.1),
        "diff_lq2": nrm((L, DIFF_QK), 0.1),
        "diff_lk2": nrm((L, DIFF_QK), 0.1),
        "diff_subln": gain((L, DIFF_VDIM)),
        "w_out": nrm((L, D_MODEL, D_MODEL), D_MODEL ** -0.5),
        "xa_norm_g": gain((L, D_MODEL)),
        "mem_norm_g": gain((L, D_MODEL)),
        "xa_wq": nrm((L, D_MODEL, D_MODEL), D_MODEL ** -0.5),
        "xa_wk": nrm((L, D_MODEL, D_MODEL), D_MODEL ** -0.5),
        "xa_wv": nrm((L, D_MODEL, D_MODEL), D_MODEL ** -0.5),
        "xa_wo": nrm((L, D_MODEL, D_MODEL), D_MODEL ** -0.5),
        "xa_q_norm": gain((L, XA_HEAD)),
        "xa_k_norm": gain((L, XA_HEAD)),
        "ffn_norm_g": gain((L, D_MODEL)),
        "ffn_w_up": nrm((L, D_MODEL, 2 * D_FF), D_MODEL ** -0.5),
        "ffn_conv_w": nrm((L, CONV_W, D_FF), CONV_W ** -0.5),
        "ffn_conv_b": nrm((L, D_FF), 0.02),
        "ffn_w_down": nrm((L, D_FF, D_MODEL), D_FF ** -0.5),
    }


def reference(x, mem, mix_norm_g, w_in, pool_w, pool_scale,
              rwkv_mu, rwkv_w0, rwkv_w2, rwkv_a0, rwkv_a2, rwkv_g2,
              rwkv_k_k, rwkv_k_a, rwkv_r_k, rwkv_ln_w, rwkv_ln_b,
              diff_q_norm, diff_k_norm, diff_lq1, diff_lk1, diff_lq2, diff_lk2, diff_subln,
              w_out, xa_norm_g, mem_norm_g, xa_wq, xa_wk, xa_wv, xa_wo, xa_q_norm, xa_k_norm,
              ffn_norm_g, ffn_w_up, ffn_conv_w, ffn_conv_b, ffn_w_down):
    h = x
    for l in range(DEPTH):
        lambda_init = 0.8 - 0.6 * math.exp(-0.3 * l)
        z = rmsnorm(h, mix_norm_g[l]) @ w_in[l]
        z_pool = z[..., :POOL_WIDTH]
        z_rwkv = z[..., POOL_WIDTH:POOL_WIDTH + RWKV_COLS]
        z_diff = z[..., POOL_WIDTH + RWKV_COLS:]
        y_pool = pool_mixer(z_pool, pool_w[l], pool_scale[l])
        y_rwkv = rwkv7_mixer(z_rwkv, rwkv_mu[l], rwkv_w0[l], rwkv_w2[l], rwkv_a0[l],
                             rwkv_a2[l], rwkv_g2[l], rwkv_k_k[l], rwkv_k_a[l],
                             rwkv_r_k[l], rwkv_ln_w[l], rwkv_ln_b[l])
        y_diff = diff_attention(z_diff, diff_q_norm[l], diff_k_norm[l], diff_lq1[l],
                                diff_lk1[l], diff_lq2[l], diff_lk2[l], diff_subln[l],
                                lambda_init)
        y = jnp.concatenate([y_pool, y_rwkv, y_diff], axis=-1)
        h = h + y @ w_out[l]
        h = h + memory_cross_attention(rmsnorm(h, xa_norm_g[l]), rmsnorm(mem, mem_norm_g[l]),
                                       xa_wq[l], xa_wk[l], xa_wv[l], xa_wo[l],
                                       xa_q_norm[l], xa_k_norm[l])
        h = h + conv_ffn(rmsnorm(h, ffn_norm_g[l]), ffn_w_up[l], ffn_conv_w[l],
                         ffn_conv_b[l], ffn_w_down[l])
    return h
```

```python
import functools
import math

import jax
import jax.numpy as jnp
import numpy as np
from jax import lax
from jax.experimental import pallas as pl
from jax.experimental.pallas import tpu as pltpu

F32 = jnp.float32
BF16 = jnp.bfloat16

D_MODEL = 1024
EPS = 1e-6
CHUNK = 64

POOL_WIDTH = 256
POOL_GDIM = 64
POOL_WINDOWS = (2, 4, 8, 16)
POOL_HALO = 16

RWKV_WIDTH = 256
RWKV_HEAD = 64
RWKV_COLS = 1024
RWKV_GN_EPS = 64e-5
RWKV_CHUNK = 64

DIFF_WIDTH = 512
DIFF_HEADS = 4
DIFF_VDIM = 128
DIFF_QK = 64
P_IN = POOL_WIDTH + RWKV_COLS + 3 * DIFF_WIDTH

XA_HEADS = 4
XA_HEAD = 256
D_FF = 2816
CONV_W = 3

LOG2E = math.log2(math.e)
NEG_BIG = -1e30

V7X_SUBLANES = 8
V7X_VMEM_LIMIT = 52 * 1024 * 1024


def _cparams(sem):
    return pltpu.CompilerParams(dimension_semantics=sem, vmem_limit_bytes=V7X_VMEM_LIMIT)


def _dot(a, b):
    return jnp.dot(a, b, preferred_element_type=F32)


def _dot_nt(a, b):
    return lax.dot_general(a, b, (((1,), (1,)), ((), ())), preferred_element_type=F32)


def _dot_tn(a, b):
    return lax.dot_general(a, b, (((0,), (0,)), ((), ())), preferred_element_type=F32)


def _group_ones(width, group):
    idx = np.arange(width) // group
    return jnp.asarray((idx[:, None] == idx[None, :]).astype(np.float32), dtype=BF16)


def _gsum1(x, ones_bd):
    w = ones_bd.shape[0]
    parts = [_dot(x[:, i:i + w].astype(BF16), ones_bd) for i in range(0, x.shape[1], w)]
    return parts[0] if len(parts) == 1 else jnp.concatenate(parts, axis=1)


def _gsum2(x, ones_bd):
    hi = x.astype(BF16)
    lo = (x - hi.astype(F32)).astype(BF16)
    return _dot(hi, ones_bd) + _dot(lo, ones_bd)


def _mix_in_kernel(x_ref, g_ref, w_ref, ones_ref, qg_ref, kg_ref,
                   zp_ref, zr_ref, q_ref, k_ref, v_ref):
    x = x_ref[0]
    ms = jnp.mean(x * x, axis=-1, keepdims=True)
    xn = (x * lax.rsqrt(ms + EPS) * g_ref[...]).astype(BF16)
    z = _dot(xn, w_ref[...])
    zp_ref[0] = z[:, :POOL_WIDTH]
    zr_ref[0] = z[:, POOL_WIDTH:POOL_WIDTH + RWKV_COLS]
    o = POOL_WIDTH + RWKV_COLS
    q = z[:, o:o + DIFF_WIDTH]
    k = z[:, o + DIFF_WIDTH:o + 2 * DIFF_WIDTH]
    v = z[:, o + 2 * DIFF_WIDTH:o + 3 * DIFF_WIDTH]
    ones_bd = ones_ref[...]
    qss = _gsum1(q * q, ones_bd) * (1.0 / DIFF_QK)
    kss = _gsum1(k * k, ones_bd) * (1.0 / DIFF_QK)
    q_ref[0] = (q * lax.rsqrt(qss + EPS) * qg_ref[...]).astype(BF16)
    k_ref[0] = (k * lax.rsqrt(kss + EPS) * kg_ref[...]).astype(BF16)
    v_ref[0] = v.astype(BF16)


def _mix_in(h, g, w_in_bf, qgain, kgain, tm):
    B, S, D = h.shape
    ones_bd = _group_ones(256, DIFF_QK)
    const = lambda shape: pl.BlockSpec(shape, lambda b, i: (0,) * len(shape))
    tile = lambda c: pl.BlockSpec((1, tm, c), lambda b, i: (b, i, 0))
    return pl.pallas_call(
        _mix_in_kernel,
        grid=(B, S // tm),
        in_specs=[tile(D), const((1, D)), const((D, P_IN)), const((256, 256)),
                  const((1, DIFF_WIDTH)), const((1, DIFF_WIDTH))],
        out_specs=[tile(POOL_WIDTH), tile(RWKV_COLS), tile(DIFF_WIDTH), tile(DIFF_WIDTH),
                   tile(DIFF_WIDTH)],
        out_shape=[jax.ShapeDtypeStruct((B, S, POOL_WIDTH), F32),
                   jax.ShapeDtypeStruct((B, S, RWKV_COLS), F32),
                   jax.ShapeDtypeStruct((B, S, DIFF_WIDTH), BF16),
                   jax.ShapeDtypeStruct((B, S, DIFF_WIDTH), BF16),
                   jax.ShapeDtypeStruct((B, S, DIFF_WIDTH), BF16)],
        compiler_params=_cparams(("parallel", "parallel")),
        name="mix_in",
    )(h, g, w_in_bf, ones_bd, qgain, kgain)


def _pool_kernel(z_ref, halo_ref, w_ref, scale_ref, y_ref, buf_ref, *, tm):
    i = pl.program_id(1)
    u = z_ref[0]
    pad = V7X_SUBLANES
    n = tm + POOL_HALO
    halo = jnp.where(i == 0, 0.0, halo_ref[0])
    buf_ref[0:pad, :] = jnp.zeros((pad, POOL_WIDTH), F32)
    buf_ref[pad:pad + POOL_HALO, :] = halo
    buf_ref[pad + POOL_HALO:pad + n, :] = u
    lane = lax.broadcasted_iota(jnp.int32, (tm, POOL_WIDTH), 1)
    grp = lane // POOL_GDIM
    win = jnp.zeros((tm, POOL_WIDTH), F32)
    shift = 1
    for gi, w in enumerate(POOL_WINDOWS):
        while shift < w:
            cur = buf_ref[pad:pad + n, :] + buf_ref[pad - shift:pad - shift + n, :]
            buf_ref[pad:pad + n, :] = cur
            shift *= 2
        win = jnp.where(grp == gi, buf_ref[pad + POOL_HALO:pad + n, :], win)
    t = i * tm + lax.broadcasted_iota(jnp.int32, (tm, POOL_WIDTH), 0)
    wlane = jnp.left_shift(2, grp)
    count = jnp.minimum(t + 1, wlane).astype(F32)
    d = win / count - u
    y = _dot(d.astype(BF16), w_ref[...]) * scale_ref[...]
    y_ref[0] = y.astype(BF16)


def _pool(z_pool, w_bd_bf, scale, tm):
    B, S, _ = z_pool.shape
    r = tm // POOL_HALO
    return pl.pallas_call(
        functools.partial(_pool_kernel, tm=tm),
        grid=(B, S // tm),
        in_specs=[pl.BlockSpec((1, tm, POOL_WIDTH), lambda b, i: (b, i, 0)),
                  pl.BlockSpec((1, POOL_HALO, POOL_WIDTH),
                               lambda b, i: (b, jnp.maximum(i * r - 1, 0), 0)),
                  pl.BlockSpec((POOL_WIDTH, POOL_WIDTH), lambda b, i: (0, 0)),
                  pl.BlockSpec((1, POOL_WIDTH), lambda b, i: (0, 0))],
        out_specs=pl.BlockSpec((1, tm, POOL_WIDTH), lambda b, i: (b, i, 0)),
        out_shape=jax.ShapeDtypeStruct((B, S, POOL_WIDTH), BF16),
        scratch_shapes=[pltpu.VMEM((V7X_SUBLANES + POOL_HALO + tm, POOL_WIDTH), F32)],
        compiler_params=_cparams(("parallel", "parallel")),
        name="pool",
    )(z_pool, z_pool, w_bd_bf, scale)


def _stack_heads(x, head_masks):
    return jnp.concatenate([jnp.where(m, x, 0.0) for m in head_masks], axis=0).astype(BF16)


def _unstack_heads(x_sm, rows):
    out = x_sm[0:rows]
    for h in range(1, RWKV_WIDTH // RWKV_HEAD):
        out = out + x_sm[h * rows:(h + 1) * rows]
    return out


def _rwkv_kernel(z_ref, halo_ref, mu_ref, w0_ref, w2_ref, a0_ref, a2_ref, g2_ref,
                 kk_ref, ka_ref, rk_ref, lnw_ref, lnb_ref, ones_ref, tri_ref,
                 strict_ref, incl_ref, eye_ref,
                 y_ref,
                 zbuf, r_s, k_s, v_s, lw_s, a_s, b_s, y_s, state, *, tm):
    i = pl.program_id(1)
    L = RWKV_CHUNK
    W = RWKV_WIDTH
    pad = V7X_SUBLANES

    @pl.when(i == 0)
    def _():
        state[...] = jnp.zeros_like(state)

    z = z_ref[0]
    zbuf[0:pad, :] = jnp.where(i == 0, 0.0, halo_ref[0])
    zbuf[pad:pad + tm, :] = z
    zprev = zbuf[pad - 1:pad - 1 + tm, :]
    zm = z + mu_ref[...] * (zprev - z)
    r = zm[:, 0:W]
    k = zm[:, W:2 * W]
    v = zm[:, 2 * W:3 * W]
    z6 = zm[:, 3 * W:3 * W + 128]
    gd = zm[:, 3 * W + 128:3 * W + 256]
    ones_bd = ones_ref[...]
    wl = w0_ref[...] + _dot(jnp.tanh(z6).astype(BF16), w2_ref[...])
    w = -jax.nn.softplus(-wl) - 0.5
    a = jax.nn.sigmoid(a0_ref[...] + _dot(z6.astype(BF16), a2_ref[...]))
    g = _dot(jax.nn.sigmoid(gd).astype(BF16), g2_ref[...])
    kk = k * kk_ref[...]
    kk = kk / jnp.maximum(jnp.sqrt(_gsum2(kk * kk, ones_bd)), 1e-12)
    kp = k * (1.0 + (a - 1.0) * ka_ref[...])
    r_s[...] = r
    k_s[...] = kp
    v_s[...] = v
    lw_s[...] = -jnp.exp(w)
    a_s[...] = -kk
    b_s[...] = kk * a

    lane = lax.broadcasted_iota(jnp.int32, (L, W), 1)
    head_masks = [(lane // RWKV_HEAD) == h for h in range(W // RWKV_HEAD)]
    strict = strict_ref[...] > 0.5
    incl = incl_ref[...] > 0.5
    eye = eye_ref[...] > 0.5

    def chunk(ci, carry):
        sl = pl.ds(pl.multiple_of(ci * L, L), L)
        lw = lw_s[sl, :]
        rr, kc, vc, ac, bc = r_s[sl, :], k_s[sl, :], v_s[sl, :], a_s[sl, :], b_s[sl, :]
        c_in = jnp.dot(tri_ref[...], lw, precision=lax.Precision.HIGHEST,
                       preferred_element_type=F32)
        c_ex = c_in - lw
        c_tot = c_in[L - 1:L, :]
        e_in = jnp.exp(c_in)
        e_neg = jnp.exp(-c_in)
        e_rem = jnp.exp(c_tot - c_in)
        at = _stack_heads(ac * jnp.exp(c_ex), head_masks)
        rt_f = rr * e_in
        rt = _stack_heads(rt_f, head_masks)
        bt = _stack_heads(bc * e_neg, head_masks)
        kt = _stack_heads(kc * e_neg, head_masks)
        bbar = _stack_heads(bc * e_rem, head_masks)
        kbar = _stack_heads(kc * e_rem, head_masks)
        vs = _stack_heads(vc, head_masks)

        tab = jnp.where(strict, _dot_nt(at, bt), 0.0)
        tak = jnp.where(strict, _dot_nt(at, kt), 0.0).astype(BF16)
        trb = jnp.where(incl, _dot_nt(rt, bt), 0.0).astype(BF16)
        trk = jnp.where(incl, _dot_nt(rt, kt), 0.0).astype(BF16)

        p = tab.astype(BF16)
        winv = jnp.where(eye, 1.0, 0.0) + tab
        steps = int(math.log2(L)) - 1
        for s in range(steps):
            p_f = _dot(p, p)
            p = p_f.astype(BF16)
            winv = winv + _dot(winv.astype(BF16), p)
        winv = winv.astype(BF16)

        takv = _dot(tak, vs)
        trkv = _dot(trk, vs)
        ahat = _dot(winv, at)
        uloc = _dot(winv, takv.astype(BF16))
        ahat_b = ahat.astype(BF16)
        uloc_b = uloc.astype(BF16)
        rhat = _unstack_heads(_dot(trb, ahat_b), L) + rt_f
        yloc = _unstack_heads(_dot(trb, uloc_b) + trkv, L)
        gmat = _dot_tn(bbar, ahat_b) + jnp.where(eye, jnp.exp(c_tot), 0.0)
        mloc = _dot_tn(bbar, uloc_b) + _dot_tn(kbar, vs)

        m0 = state[...].astype(BF16)
        y_s[sl, :] = _dot(rhat.astype(BF16), m0) + yloc
        state[...] = _dot(gmat.astype(BF16), m0) + mloc
        return carry

    lax.fori_loop(0, tm // L, chunk, 0)

    y = y_s[...]
    r = r_s[...]
    kp = k_s[...]
    v = v_s[...]
    inv_n = 1.0 / RWKV_HEAD
    mean = _gsum2(y, ones_bd) * inv_n
    yc = y - mean
    var = _gsum2(yc * yc, ones_bd) * inv_n
    yn = yc * lax.rsqrt(var + RWKV_GN_EPS) * lnw_ref[...] + lnb_ref[...]
    bonus = _gsum2(r * kp * rk_ref[...], ones_bd) * v
    y_ref[0] = ((yn + bonus) * g).astype(BF16)


def _rwkv_masks():
    L = RWKV_CHUNK
    n = RWKV_WIDTH
    row = np.arange(n)
    same = (row[:, None] // L) == (row[None, :] // L)
    tr = row[:, None] % L
    tc = row[None, :] % L
    strict = (same & (tr > tc)).astype(np.float32)
    incl = (same & (tr >= tc)).astype(np.float32)
    eye = np.eye(n, dtype=np.float32)
    tri = (np.arange(L)[:, None] >= np.arange(L)[None, :]).astype(np.float32)
    return (jnp.asarray(tri), jnp.asarray(strict), jnp.asarray(incl), jnp.asarray(eye))


def _rwkv(z_rwkv, mu, w0, w2p, a0, a2p, g2, k_k, k_a, r_k, ln_w, ln_b, tm):
    B, S, C = z_rwkv.shape
    W = RWKV_WIDTH
    ones_bd = _group_ones(W, RWKV_HEAD)
    tri, strict, incl, eye = _rwkv_masks()
    r = tm // V7X_SUBLANES
    const = lambda shape: pl.BlockSpec(shape, lambda b, i: (0,) * len(shape))
    vec = const((1, W))
    sq = const((W, W))
    return pl.pallas_call(
        functools.partial(_rwkv_kernel, tm=tm),
        grid=(B, S // tm),
        in_specs=[pl.BlockSpec((1, tm, C), lambda b, i: (b, i, 0)),
                  pl.BlockSpec((1, V7X_SUBLANES, C),
                               lambda b, i: (b, jnp.maximum(i * r - 1, 0), 0)),
                  const((1, C)), vec, const((128, W)), vec, const((128, W)), const((128, W)),
                  vec, vec, vec, vec, vec, sq, const((RWKV_CHUNK, RWKV_CHUNK)), sq, sq, sq],
        out_specs=pl.BlockSpec((1, tm, W), lambda b, i: (b, i, 0)),
        out_shape=jax.ShapeDtypeStruct((B, S, W), BF16),
        scratch_shapes=[pltpu.VMEM((V7X_SUBLANES + tm, C), F32)]
        + [pltpu.VMEM((tm, W), F32) for _ in range(7)]
        + [pltpu.VMEM((W, W), F32)],
        compiler_params=_cparams(("parallel", "arbitrary")),
        name="rwkv",
    )(z_rwkv, z_rwkv, mu, w0, w2p, a0, a2p, g2, k_k, k_a, r_k, ln_w, ln_b, ones_bd, tri,
      strict, incl, eye)


def _diff_kernel(q_ref, k_ref, v_ref, lq1_ref, lk1_ref, lq2_ref, lk2_ref, subln_ref,
                 o_ref, dbias, m_s, l_s, acc_s, *, t, lambda_init):
    h = pl.program_id(1)
    i = pl.program_id(2)
    slope2 = jnp.float32(0.0)
    for hh in range(DIFF_HEADS):
        slope2 = jnp.where(h == hh, 2.0 ** (-8.0 * (hh + 1) / DIFF_HEADS) * LOG2E, slope2)

    @pl.when(i == 0)
    def _():
        rr = lax.broadcasted_iota(jnp.int32, (t, t), 0)
        cc = lax.broadcasted_iota(jnp.int32, (t, t), 1)
        vis = (cc // CHUNK) <= (rr // CHUNK)
        rel = (rr - jnp.abs(rr - cc)).astype(F32)
        dbias[...] = jnp.where(vis, slope2 * rel, NEG_BIG)

    q = q_ref[0]
    lane = lax.broadcasted_iota(jnp.int32, (t, 2 * DIFF_QK), 1)
    qc = [jnp.where(lane < DIFF_QK, q, jnp.zeros_like(q)),
          jnp.where(lane >= DIFF_QK, q, jnp.zeros_like(q))]

    m_s[...] = jnp.full(m_s.shape, NEG_BIG, F32)
    l_s[...] = jnp.zeros_like(l_s)
    acc_s[...] = jnp.zeros_like(acc_s)

    def step(kt, vt, bias):
        for c in range(2):
            s = _dot_nt(qc[c], kt) + bias
            m_old = m_s[c]
            m_new = jnp.maximum(m_old, jnp.max(s, axis=-1, keepdims=True))
            alpha = jnp.exp2(m_old - m_new)
            p = jnp.exp2(s - m_new)
            l_s[c] = alpha * l_s[c] + jnp.sum(p, axis=-1, keepdims=True)
            acc_s[c] = alpha * acc_s[c] + _dot(p.astype(BF16), vt)
            m_s[c] = m_new

    def body(j, carry):
        sl = pl.ds(pl.multiple_of(j * t, t), t)
        col = lax.broadcasted_iota(jnp.int32, (1, t), 1)
        bias = slope2 * ((j - i) * t + col).astype(F32)
        step(k_ref[0, sl, :], v_ref[0, sl, :], bias)
        return carry

    lax.fori_loop(0, i, body, 0)
    sl = pl.ds(pl.multiple_of(i * t, t), t)
    step(k_ref[0, sl, :], v_ref[0, sl, :], dbias[...])

    lam = (jnp.exp(jnp.sum(lq1_ref[...] * lk1_ref[...], axis=-1, keepdims=True))
           - jnp.exp(jnp.sum(lq2_ref[...] * lk2_ref[...], axis=-1, keepdims=True))
           + lambda_init)
    o = acc_s[0] / l_s[0] - lam * (acc_s[1] / l_s[1])
    ms = jnp.mean(o * o, axis=-1, keepdims=True)
    o = o * lax.rsqrt(ms + EPS) * subln_ref[...] * (1.0 - lambda_init)
    o_ref[0] = o.astype(BF16)


def _diffattn(q, k, v, lq1, lk1, lq2, lk2, subln, lambda_init, t):
    B, S, _ = q.shape
    hw = 2 * DIFF_QK
    vec = pl.BlockSpec((1, DIFF_QK), lambda b, h, i: (0, 0))
    return pl.pallas_call(
        functools.partial(_diff_kernel, t=t, lambda_init=lambda_init),
        grid=(B, DIFF_HEADS, S // t),
        in_specs=[pl.BlockSpec((1, t, hw), lambda b, h, i: (b, i, h)),
                  pl.BlockSpec((1, S, hw), lambda b, h, i: (b, 0, h)),
                  pl.BlockSpec((1, S, DIFF_VDIM), lambda b, h, i: (b, 0, h)),
                  vec, vec, vec, vec,
                  pl.BlockSpec((1, DIFF_VDIM), lambda b, h, i: (0, 0))],
        out_specs=pl.BlockSpec((1, t, DIFF_VDIM), lambda b, h, i: (b, i, h)),
        out_shape=jax.ShapeDtypeStruct((B, S, DIFF_WIDTH), BF16),
        scratch_shapes=[pltpu.VMEM((t, t), F32),
                        pltpu.VMEM((2, t, 1), F32),
                        pltpu.VMEM((2, t, 1), F32),
                        pltpu.VMEM((2, t, DIFF_VDIM), F32)],
        compiler_params=_cparams(("parallel", "arbitrary", "arbitrary")),
        name="diffattn",
    )(q, k, v, lq1, lk1, lq2, lk2, subln)


def _memkv_kernel(mem_ref, g_ref, wk_ref, wv_ref, kn_ref, k_ref, v_ref):
    x = mem_ref[0]
    ms = jnp.mean(x * x, axis=-1, keepdims=True)
    xn = (x * lax.rsqrt(ms + EPS) * g_ref[...]).astype(BF16)
    k = _dot(xn, wk_ref[...])
    v = _dot(xn, wv_ref[...])
    ks = []
    for h in range(XA_HEADS):
        kh = k[:, h * XA_HEAD:(h + 1) * XA_HEAD]
        kms = jnp.mean(kh * kh, axis=-1, keepdims=True)
        ks.append(kh * lax.rsqrt(kms + EPS) * kn_ref[...])
    k_ref[0] = jnp.concatenate(ks, axis=1).astype(BF16)
    v_ref[0] = v.astype(BF16)


def _memkv(mem, g, wk_bf, wv_bf, k_norm):
    B, M, D = mem.shape
    const = lambda shape: pl.BlockSpec(shape, lambda b: (0,) * len(shape))
    tile = pl.BlockSpec((1, M, D), lambda b: (b, 0, 0))
    return pl.pallas_call(
        _memkv_kernel,
        grid=(B,),
        in_specs=[tile, const((1, D)), const((D, D)), const((D, D)), const((1, XA_HEAD))],
        out_specs=[tile, tile],
        out_shape=[jax.ShapeDtypeStruct((B, M, D), BF16), jax.ShapeDtypeStruct((B, M, D), BF16)],
        compiler_params=_cparams(("parallel",)),
        name="memkv",
    )(mem, g, wk_bf, wv_bf, k_norm)


def _mid_kernel(h_ref, yp_ref, yr_ref, yd_ref, wout_ref, g_ref, wq_ref, km_ref, vm_ref,
                wo_ref, qn_ref, o_ref):
    a = POOL_WIDTH
    b = POOL_WIDTH + RWKV_WIDTH
    h1 = (h_ref[0] + _dot(yp_ref[0], wout_ref[0:a, :]) + _dot(yr_ref[0], wout_ref[a:b, :])
          + _dot(yd_ref[0], wout_ref[b:D_MODEL, :]))
    ms = jnp.mean(h1 * h1, axis=-1, keepdims=True)
    xn = (h1 * lax.rsqrt(ms + EPS) * g_ref[...]).astype(BF16)
    q = _dot(xn, wq_ref[...])
    qscale = (XA_HEAD ** -0.5) * LOG2E
    outs = []
    for hd in range(XA_HEADS):
        sl = slice(hd * XA_HEAD, (hd + 1) * XA_HEAD)
        qh = q[:, sl]
        qms = jnp.mean(qh * qh, axis=-1, keepdims=True)
        qh = (qh * lax.rsqrt(qms + EPS) * (qn_ref[...] * qscale)).astype(BF16)
        s = _dot_nt(qh, km_ref[0, :, sl])
        m = jnp.max(s, axis=-1, keepdims=True)
        p = jnp.exp2(s - m)
        l = jnp.sum(p, axis=-1, keepdims=True)
        outs.append(_dot(p.astype(BF16), vm_ref[0, :, sl]) / l)
    o = jnp.concatenate(outs, axis=1).astype(BF16)
    o_ref[0] = h1 + _dot(o, wo_ref[...])


def _mid(h, yp, yr, yd, wout_bf, g, wq_bf, kmem, vmem, wo_bf, q_norm, tm):
    B, S, D = h.shape
    M = kmem.shape[1]
    const = lambda shape: pl.BlockSpec(shape, lambda b, i: (0,) * len(shape))
    tile = lambda c: pl.BlockSpec((1, tm, c), lambda b, i: (b, i, 0))
    memspec = pl.BlockSpec((1, M, D), lambda b, i: (b, 0, 0))
    return pl.pallas_call(
        _mid_kernel,
        grid=(B, S // tm),
        in_specs=[tile(D), tile(POOL_WIDTH), tile(RWKV_WIDTH), tile(DIFF_WIDTH), const((D, D)),
                  const((1, D)), const((D, D)), memspec, memspec, const((D, D)),
                  const((1, XA_HEAD))],
        out_specs=tile(D),
        out_shape=jax.ShapeDtypeStruct((B, S, D), F32),
        compiler_params=_cparams(("parallel", "parallel")),
        name="mid",
    )(h, yp, yr, yd, wout_bf, g, wq_bf, kmem, vmem, wo_bf, q_norm)


def _ffn_kernel(h_ref, g_ref, wa_ref, wb_ref, cw_ref, cb_ref, wd_ref, o_ref,
                xn_s, abuf, carry, *, tm):
    i = pl.program_id(1)
    j = pl.program_id(2)
    pad = V7X_SUBLANES

    @pl.when(j == 0)
    def _():
        x = h_ref[0]
        ms = jnp.mean(x * x, axis=-1, keepdims=True)
        xn_s[...] = (x * lax.rsqrt(ms + EPS) * g_ref[...]).astype(BF16)

    xn = xn_s[...]
    a = _dot(xn, wa_ref[...])
    b = _dot(xn, wb_ref[...])
    abuf[0:pad, :] = jnp.where(i == 0, 0.0, carry[j])
    abuf[pad:pad + tm, :] = a
    carry[j] = a[tm - pad:tm, :]
    cw = cw_ref[...]
    c = (cw[2:3, :] * a + cw[1:2, :] * abuf[pad - 1:pad - 1 + tm, :]
         + cw[0:1, :] * abuf[pad - 2:pad - 2 + tm, :] + cb_ref[...])
    gelu = 0.5 * c * (1.0 + lax.erf(c * (2.0 ** -0.5)))
    hmid = (gelu * b).astype(BF16)
    contrib = _dot(hmid, wd_ref[...])

    @pl.when(j == 0)
    def _():
        o_ref[0] = h_ref[0] + contrib

    @pl.when(j != 0)
    def _():
        o_ref[0] = o_ref[0] + contrib


def _ffn(h, g, wup_bf, conv_w, conv_b, wdown_bf, tm, nf):
    B, S, D = h.shape
    tf = D_FF // nf
    return pl.pallas_call(
        functools.partial(_ffn_kernel, tm=tm),
        grid=(B, S // tm, nf),
        in_specs=[pl.BlockSpec((1, tm, D), lambda b, i, j: (b, i, 0)),
                  pl.BlockSpec((1, D), lambda b, i, j: (0, 0)),
                  pl.BlockSpec((D, tf), lambda b, i, j: (0, j)),
                  pl.BlockSpec((D, tf), lambda b, i, j: (0, j + nf)),
                  pl.BlockSpec((CONV_W, tf), lambda b, i, j: (0, j)),
                  pl.BlockSpec((1, tf), lambda b, i, j: (0, j)),
                  pl.BlockSpec((tf, D), lambda b, i, j: (j, 0))],
        out_specs=pl.BlockSpec((1, tm, D), lambda b, i, j: (b, i, 0)),
        out_shape=jax.ShapeDtypeStruct((B, S, D), F32),
        scratch_shapes=[pltpu.VMEM((tm, D), BF16),
                        pltpu.VMEM((V7X_SUBLANES + tm, tf), F32),
                        pltpu.VMEM((nf, V7X_SUBLANES, tf), F32)],
        compiler_params=_cparams(("parallel", "arbitrary", "arbitrary")),
        name="ffn",
    )(h, g, wup_bf, wup_bf, conv_w, conv_b, wdown_bf)


def _tiles(S):
    pick = lambda pref: max(c for c in (64, 128, 256, 512, 1024) if c <= pref and S % c == 0)
    return dict(mix=pick(512), pool=pick(1024), rwkv=pick(256), attn=pick(256), mid=pick(512),
                ffn=pick(512))


def _block_diag(blocks):
    n = len(blocks)
    rows = []
    for i, blk in enumerate(blocks):
        rows.append(jnp.concatenate(
            [blk if j == i else jnp.zeros_like(blk) for j in range(n)], axis=1))
    return jnp.concatenate(rows, axis=0)


def kernel(x, mem, mix_norm_g, w_in, pool_w, pool_scale, rwkv_mu, rwkv_w0, rwkv_w2, rwkv_a0, rwkv_a2, rwkv_g2, rwkv_k_k, rwkv_k_a, rwkv_r_k, rwkv_ln_w, rwkv_ln_b, diff_q_norm, diff_k_norm, diff_lq1, diff_lk1, diff_lq2, diff_lk2, diff_subln, w_out, xa_norm_g, mem_norm_g, xa_wq, xa_wk, xa_wv, xa_wo, xa_q_norm, xa_k_norm, ffn_norm_g, ffn_w_up, ffn_conv_w, ffn_conv_b, ffn_w_down):
    B, S, D = x.shape
    depth = w_in.shape[0]
    tl = _tiles(S)
    row = lambda a: a.reshape(1, -1).astype(F32)
    h = x
    for l in range(depth):
        lambda_init = 0.8 - 0.6 * math.exp(-0.3 * l)
        qgain = row(jnp.tile(diff_q_norm[l].reshape(-1), DIFF_HEADS)) * (DIFF_QK ** -0.5 * LOG2E)
        kgain = row(jnp.tile(diff_k_norm[l].reshape(-1), DIFF_HEADS))
        zeros64 = jnp.zeros((64, RWKV_WIDTH), F32)
        w2p = jnp.concatenate([rwkv_w2[l], zeros64], axis=0).astype(BF16)
        a2p = jnp.concatenate([zeros64, rwkv_a2[l]], axis=0).astype(BF16)
        pool_bd = _block_diag([pool_w[l, gi] for gi in range(len(POOL_WINDOWS))]).astype(BF16)

        z_pool, z_rwkv, qd, kd, vd = _mix_in(h, row(mix_norm_g[l]), w_in[l].astype(BF16),
                                             qgain, kgain, tl["mix"])
        y_pool = _pool(z_pool, pool_bd, row(pool_scale[l]), tl["pool"])
        y_rwkv = _rwkv(z_rwkv, row(rwkv_mu[l]), row(rwkv_w0[l]), w2p, row(rwkv_a0[l]), a2p,
                       rwkv_g2[l].astype(BF16), row(rwkv_k_k[l]), row(rwkv_k_a[l]),
                       row(rwkv_r_k[l]), row(rwkv_ln_w[l]), row(rwkv_ln_b[l]), tl["rwkv"])
        y_diff = _diffattn(qd, kd, vd, row(diff_lq1[l]), row(diff_lk1[l]), row(diff_lq2[l]),
                           row(diff_lk2[l]), row(diff_subln[l]), lambda_init, tl["attn"])
        kmem, vmem = _memkv(mem, row(mem_norm_g[l]), xa_wk[l].astype(BF16),
                            xa_wv[l].astype(BF16), row(xa_k_norm[l]))
        h = _mid(h, y_pool, y_rwkv, y_diff, w_out[l].astype(BF16), row(xa_norm_g[l]),
                 xa_wq[l].astype(BF16), kmem, vmem, xa_wo[l].astype(BF16), row(xa_q_norm[l]),
                 tl["mid"])
        h = _ffn(h, row(ffn_norm_g[l]), ffn_w_up[l].astype(BF16), ffn_conv_w[l].astype(F32),
                 row(ffn_conv_b[l]), ffn_w_down[l].astype(BF16), tl["ffn"], 2)
    return h
```

```python
import functools
import math

import jax
import jax.numpy as jnp
import numpy as np
from jax import lax
from jax.experimental import pallas as pl
from jax.experimental.pallas import tpu as pltpu

F32 = jnp.float32
BF16 = jnp.bfloat16

D_MODEL = 1024
EPS = 1e-6
CHUNK = 64

POOL_WIDTH = 256
POOL_GDIM = 64
POOL_WINDOWS = (2, 4, 8, 16)
POOL_HALO = 16

RWKV_WIDTH = 256
RWKV_HEAD = 64
RWKV_COLS = 1024
RWKV_GN_EPS = 64e-5
RWKV_CHUNK = 64

DIFF_WIDTH = 512
DIFF_HEADS = 4
DIFF_VDIM = 128
DIFF_QK = 64
P_IN = POOL_WIDTH + RWKV_COLS + 3 * DIFF_WIDTH

XA_HEADS = 4
XA_HEAD = 256
D_FF = 2816
CONV_W = 3

LOG2E = math.log2(math.e)
NEG_BIG = -1e30
ATTN_T = 512

V7X_SUBLANES = 8
V7X_VMEM_LIMIT = 52 * 1024 * 1024


def _cparams(sem):
    return pltpu.CompilerParams(dimension_semantics=sem, vmem_limit_bytes=V7X_VMEM_LIMIT)


def _dot(a, b):
    return jnp.dot(a, b, preferred_element_type=F32)


def _dot_nt(a, b):
    return lax.dot_general(a, b, (((1,), (1,)), ((), ())), preferred_element_type=F32)


def _dot_tn(a, b):
    return lax.dot_general(a, b, (((0,), (0,)), ((), ())), preferred_element_type=F32)


def _group_ones(width, group):
    idx = np.arange(width) // group
    return jnp.asarray((idx[:, None] == idx[None, :]).astype(np.float32), dtype=BF16)


def _gsum1(x, ones_bd):
    w = ones_bd.shape[0]
    parts = [_dot(x[:, i:i + w].astype(BF16), ones_bd) for i in range(0, x.shape[1], w)]
    return parts[0] if len(parts) == 1 else jnp.concatenate(parts, axis=1)


def _gsum2(x, ones_bd):
    hi = x.astype(BF16)
    lo = (x - hi.astype(F32)).astype(BF16)
    return _dot(hi, ones_bd) + _dot(lo, ones_bd)


def _mix_in_kernel(x_ref, g_ref, w_ref, ones_ref, qg_ref, kg_ref,
                   zp_ref, zr_ref, q_ref, k_ref, v_ref, *, tm):
    x = x_ref[0]
    ms = jnp.mean(x * x, axis=-1, keepdims=True)
    xn = (x * lax.rsqrt(ms + EPS) * g_ref[...]).astype(BF16)
    z = _dot(xn, w_ref[...])
    zp_ref[0] = z[:, :POOL_WIDTH]
    zr_ref[0] = z[:, POOL_WIDTH:POOL_WIDTH + RWKV_COLS]
    o = POOL_WIDTH + RWKV_COLS
    q = z[:, o:o + DIFF_WIDTH]
    k = z[:, o + DIFF_WIDTH:o + 2 * DIFF_WIDTH]
    v = z[:, o + 2 * DIFF_WIDTH:o + 3 * DIFF_WIDTH]
    ones_bd = ones_ref[...]
    qss = _gsum1(q * q, ones_bd) * (1.0 / DIFF_QK)
    kss = _gsum1(k * k, ones_bd) * (1.0 / DIFF_QK)
    qn = q * lax.rsqrt(qss + EPS) * qg_ref[...]
    k_ref[0] = (k * lax.rsqrt(kss + EPS) * kg_ref[...]).astype(BF16)
    for n in range(tm // ATTN_T):
        rows = slice(n * ATTN_T, (n + 1) * ATTN_T)
        q_ref[0, n] = qn[rows, :].T.astype(BF16)
        v_ref[0, n] = v[rows, :].T.astype(BF16)


def _mix_in(h, g, w_in_bf, qgain, kgain, tm):
    B, S, D = h.shape
    ones_bd = _group_ones(256, DIFF_QK)
    const = lambda shape: pl.BlockSpec(shape, lambda b, i: (0,) * len(shape))
    tile = lambda c: pl.BlockSpec((1, tm, c), lambda b, i: (b, i, 0))
    tile_t = pl.BlockSpec((1, tm // ATTN_T, DIFF_WIDTH, ATTN_T), lambda b, i: (b, i, 0, 0))
    shape_t = jax.ShapeDtypeStruct((B, S // ATTN_T, DIFF_WIDTH, ATTN_T), BF16)
    return pl.pallas_call(
        functools.partial(_mix_in_kernel, tm=tm),
        grid=(B, S // tm),
        in_specs=[tile(D), const((1, D)), const((D, P_IN)), const((256, 256)),
                  const((1, DIFF_WIDTH)), const((1, DIFF_WIDTH))],
        out_specs=[tile(POOL_WIDTH), tile(RWKV_COLS), tile_t, tile(DIFF_WIDTH), tile_t],
        out_shape=[jax.ShapeDtypeStruct((B, S, POOL_WIDTH), F32),
                   jax.ShapeDtypeStruct((B, S, RWKV_COLS), F32),
                   shape_t,
                   jax.ShapeDtypeStruct((B, S, DIFF_WIDTH), BF16),
                   shape_t],
        compiler_params=_cparams(("parallel", "parallel")),
        name="mix_in",
    )(h, g, w_in_bf, ones_bd, qgain, kgain)


def _pool_kernel(z_ref, halo_ref, w_ref, scale_ref, y_ref, buf_ref, *, tm):
    i = pl.program_id(1)
    u = z_ref[0]
    pad = V7X_SUBLANES
    n = tm + POOL_HALO
    halo = jnp.where(i == 0, 0.0, halo_ref[0])
    buf_ref[0:pad, :] = jnp.zeros((pad, POOL_WIDTH), F32)
    buf_ref[pad:pad + POOL_HALO, :] = halo
    buf_ref[pad + POOL_HALO:pad + n, :] = u
    lane = lax.broadcasted_iota(jnp.int32, (tm, POOL_WIDTH), 1)
    grp = lane // POOL_GDIM
    win = jnp.zeros((tm, POOL_WIDTH), F32)
    shift = 1
    for gi, w in enumerate(POOL_WINDOWS):
        while shift < w:
            cur = buf_ref[pad:pad + n, :] + buf_ref[pad - shift:pad - shift + n, :]
            buf_ref[pad:pad + n, :] = cur
            shift *= 2
        win = jnp.where(grp == gi, buf_ref[pad + POOL_HALO:pad + n, :], win)
    t = i * tm + lax.broadcasted_iota(jnp.int32, (tm, POOL_WIDTH), 0)
    wlane = jnp.left_shift(2, grp)
    count = jnp.minimum(t + 1, wlane).astype(F32)
    d = win / count - u
    y = _dot(d.astype(BF16), w_ref[...]) * scale_ref[...]
    y_ref[0] = y.astype(BF16)


def _pool(z_pool, w_bd_bf, scale, tm):
    B, S, _ = z_pool.shape
    r = tm // POOL_HALO
    return pl.pallas_call(
        functools.partial(_pool_kernel, tm=tm),
        grid=(B, S // tm),
        in_specs=[pl.BlockSpec((1, tm, POOL_WIDTH), lambda b, i: (b, i, 0)),
                  pl.BlockSpec((1, POOL_HALO, POOL_WIDTH),
                               lambda b, i: (b, jnp.maximum(i * r - 1, 0), 0)),
                  pl.BlockSpec((POOL_WIDTH, POOL_WIDTH), lambda b, i: (0, 0)),
                  pl.BlockSpec((1, POOL_WIDTH), lambda b, i: (0, 0))],
        out_specs=pl.BlockSpec((1, tm, POOL_WIDTH), lambda b, i: (b, i, 0)),
        out_shape=jax.ShapeDtypeStruct((B, S, POOL_WIDTH), BF16),
        scratch_shapes=[pltpu.VMEM((V7X_SUBLANES + POOL_HALO + tm, POOL_WIDTH), F32)],
        compiler_params=_cparams(("parallel", "parallel")),
        name="pool",
    )(z_pool, z_pool, w_bd_bf, scale)


def _stack_heads(x, head_masks):
    return jnp.concatenate([jnp.where(m, x, 0.0) for m in head_masks], axis=0).astype(BF16)


def _unstack_heads(x_sm, rows):
    out = x_sm[0:rows]
    for h in range(1, RWKV_WIDTH // RWKV_HEAD):
        out = out + x_sm[h * rows:(h + 1) * rows]
    return out


def _rwkv_kernel(z_ref, halo_ref, mu_ref, w0_ref, w2_ref, a0_ref, a2_ref, g2_ref,
                 kk_ref, ka_ref, rk_ref, lnw_ref, lnb_ref, ones_ref, tri_ref,
                 strict_ref, incl_ref, eye_ref,
                 y_ref,
                 zbuf, r_s, k_s, v_s, lw_s, a_s, b_s, y_s, state, rh_s, g_s, ml_s, *, tm):
    i = pl.program_id(1)
    L = RWKV_CHUNK
    W = RWKV_WIDTH
    pad = V7X_SUBLANES

    @pl.when(i == 0)
    def _():
        state[...] = jnp.zeros_like(state)

    z = z_ref[0]
    zbuf[0:pad, :] = jnp.where(i == 0, 0.0, halo_ref[0])
    zbuf[pad:pad + tm, :] = z
    zprev = zbuf[pad - 1:pad - 1 + tm, :]
    zm = z + mu_ref[...] * (zprev - z)
    r = zm[:, 0:W]
    k = zm[:, W:2 * W]
    v = zm[:, 2 * W:3 * W]
    z6 = zm[:, 3 * W:3 * W + 128]
    gd = zm[:, 3 * W + 128:3 * W + 256]
    ones_bd = ones_ref[...]
    wl = w0_ref[...] + _dot(jnp.tanh(z6).astype(BF16), w2_ref[...])
    w = -jax.nn.softplus(-wl) - 0.5
    a = jax.nn.sigmoid(a0_ref[...] + _dot(z6.astype(BF16), a2_ref[...]))
    g = _dot(jax.nn.sigmoid(gd).astype(BF16), g2_ref[...])
    kk = k * kk_ref[...]
    kk = kk / jnp.maximum(jnp.sqrt(_gsum2(kk * kk, ones_bd)), 1e-12)
    kp = k * (1.0 + (a - 1.0) * ka_ref[...])
    r_s[...] = r
    k_s[...] = kp
    v_s[...] = v
    lw_s[...] = -jnp.exp(w)
    a_s[...] = -kk
    b_s[...] = kk * a

    lane = lax.broadcasted_iota(jnp.int32, (L, W), 1)
    head_masks = [(lane // RWKV_HEAD) == h for h in range(W // RWKV_HEAD)]
    strict = strict_ref[...] > 0.5
    incl = incl_ref[...] > 0.5
    eye = eye_ref[...] > 0.5

    def chunk_local(ci):
        sl = slice(ci * L, (ci + 1) * L)
        lw = lw_s[sl, :]
        rr, kc, vc, ac, bc = r_s[sl, :], k_s[sl, :], v_s[sl, :], a_s[sl, :], b_s[sl, :]
        c_in = jnp.dot(tri_ref[...], lw, precision=lax.Precision.HIGHEST,
                       preferred_element_type=F32)
        c_ex = c_in - lw
        c_tot = c_in[L - 1:L, :]
        e_in = jnp.exp(c_in)
        e_neg = jnp.exp(-c_in)
        e_rem = jnp.exp(c_tot - c_in)
        at = _stack_heads(ac * jnp.exp(c_ex), head_masks)
        rt_f = rr * e_in
        rt = _stack_heads(rt_f, head_masks)
        bt = _stack_heads(bc * e_neg, head_masks)
        kt = _stack_heads(kc * e_neg, head_masks)
        bbar = _stack_heads(bc * e_rem, head_masks)
        kbar = _stack_heads(kc * e_rem, head_masks)
        vs = _stack_heads(vc, head_masks)

        tab = jnp.where(strict, _dot_nt(at, bt), 0.0)
        tak = jnp.where(strict, _dot_nt(at, kt), 0.0).astype(BF16)
        trb = jnp.where(incl, _dot_nt(rt, bt), 0.0).astype(BF16)
        trk = jnp.where(incl, _dot_nt(rt, kt), 0.0).astype(BF16)

        p = tab.astype(BF16)
        winv = jnp.where(eye, 1.0, 0.0) + tab
        steps = int(math.log2(L)) - 1
        for s in range(steps):
            p_f = _dot(p, p)
            p = p_f.astype(BF16)
            winv = winv + _dot(winv.astype(BF16), p)
        winv = winv.astype(BF16)

        takv = _dot(tak, vs)
        trkv = _dot(trk, vs)
        ahat = _dot(winv, at)
        uloc = _dot(winv, takv.astype(BF16))
        ahat_b = ahat.astype(BF16)
        uloc_b = uloc.astype(BF16)
        rhat = _unstack_heads(_dot(trb, ahat_b), L) + rt_f
        yloc = _unstack_heads(_dot(trb, uloc_b) + trkv, L)
        gmat = _dot_tn(bbar, ahat_b) + jnp.where(eye, jnp.exp(c_tot), 0.0)
        mloc = _dot_tn(bbar, uloc_b) + _dot_tn(kbar, vs)
        rh_s[sl, :] = rhat.astype(BF16)
        y_s[sl, :] = yloc
        g_s[ci] = gmat.astype(BF16)
        ml_s[ci] = mloc

    n_chunks = tm // L
    for ci in range(n_chunks):
        chunk_local(ci)
    m_cur = state[...]
    for ci in range(n_chunks):
        sl = slice(ci * L, (ci + 1) * L)
        m0 = m_cur.astype(BF16)
        y_s[sl, :] = y_s[sl, :] + _dot(rh_s[sl, :], m0)
        m_cur = _dot(g_s[ci], m0) + ml_s[ci]
    state[...] = m_cur

    y = y_s[...]
    r = r_s[...]
    kp = k_s[...]
    v = v_s[...]
    inv_n = 1.0 / RWKV_HEAD
    mean = _gsum2(y, ones_bd) * inv_n
    yc = y - mean
    var = _gsum2(yc * yc, ones_bd) * inv_n
    yn = yc * lax.rsqrt(var + RWKV_GN_EPS) * lnw_ref[...] + lnb_ref[...]
    bonus = _gsum2(r * kp * rk_ref[...], ones_bd) * v
    y_ref[0] = ((yn + bonus) * g).astype(BF16)


def _rwkv_masks():
    L = RWKV_CHUNK
    n = RWKV_WIDTH
    row = np.arange(n)
    same = (row[:, None] // L) == (row[None, :] // L)
    tr = row[:, None] % L
    tc = row[None, :] % L
    strict = (same & (tr > tc)).astype(np.float32)
    incl = (same & (tr >= tc)).astype(np.float32)
    eye = np.eye(n, dtype=np.float32)
    tri = (np.arange(L)[:, None] >= np.arange(L)[None, :]).astype(np.float32)
    return (jnp.asarray(tri), jnp.asarray(strict), jnp.asarray(incl), jnp.asarray(eye))


def _rwkv(z_rwkv, mu, w0, w2p, a0, a2p, g2, k_k, k_a, r_k, ln_w, ln_b, tm):
    B, S, C = z_rwkv.shape
    W = RWKV_WIDTH
    ones_bd = _group_ones(W, RWKV_HEAD)
    tri, strict, incl, eye = _rwkv_masks()
    r = tm // V7X_SUBLANES
    const = lambda shape: pl.BlockSpec(shape, lambda b, i: (0,) * len(shape))
    vec = const((1, W))
    sq = const((W, W))
    return pl.pallas_call(
        functools.partial(_rwkv_kernel, tm=tm),
        grid=(B, S // tm),
        in_specs=[pl.BlockSpec((1, tm, C), lambda b, i: (b, i, 0)),
                  pl.BlockSpec((1, V7X_SUBLANES, C),
                               lambda b, i: (b, jnp.maximum(i * r - 1, 0), 0)),
                  const((1, C)), vec, const((128, W)), vec, const((128, W)), const((128, W)),
                  vec, vec, vec, vec, vec, sq, const((RWKV_CHUNK, RWKV_CHUNK)), sq, sq, sq],
        out_specs=pl.BlockSpec((1, tm, W), lambda b, i: (b, i, 0)),
        out_shape=jax.ShapeDtypeStruct((B, S, W), BF16),
        scratch_shapes=[pltpu.VMEM((V7X_SUBLANES + tm, C), F32)]
        + [pltpu.VMEM((tm, W), F32) for _ in range(7)]
        + [pltpu.VMEM((W, W), F32),
           pltpu.VMEM((tm, W), BF16),
           pltpu.VMEM((tm // RWKV_CHUNK, W, W), BF16),
           pltpu.VMEM((tm // RWKV_CHUNK, W, W), F32)],
        compiler_params=_cparams(("parallel", "arbitrary")),
        name="rwkv",
    )(z_rwkv, z_rwkv, mu, w0, w2p, a0, a2p, g2, k_k, k_a, r_k, ln_w, ln_b, ones_bd, tri,
      strict, incl, eye)


DIFF_ONES_ROWS = 16


def _diff_kernel(qt_ref, k_ref, vt_ref, lq1_ref, lk1_ref, lq2_ref, lk2_ref, subln_ref,
                 o_ref, dbias, obias, m_s, acc_s, s_a, s_b, *, lambda_init):
    t = ATTN_T
    h = pl.program_id(1)
    i = pl.program_id(2)
    slope2 = jnp.float32(0.0)
    for hh in range(DIFF_HEADS):
        slope2 = jnp.where(h == hh, 2.0 ** (-8.0 * (hh + 1) / DIFF_HEADS) * LOG2E, slope2)

    @pl.when(i == 0)
    def _():
        kc = lax.broadcasted_iota(jnp.int32, (t, t), 0)
        qr = lax.broadcasted_iota(jnp.int32, (t, t), 1)
        vis = (kc // CHUNK) <= (qr // CHUNK)
        rel = (qr - jnp.abs(qr - kc)).astype(F32)
        dbias[...] = jnp.where(vis, slope2 * rel, NEG_BIG)
        obias[...] = slope2 * kc.astype(F32)

    qt = qt_ref[0, 0]
    row = lax.broadcasted_iota(jnp.int32, (2 * DIFF_QK, t), 0)
    qc = [jnp.where(row < DIFF_QK, qt, jnp.zeros_like(qt)),
          jnp.where(row >= DIFF_QK, qt, jnp.zeros_like(qt))]
    ones = jnp.ones((DIFF_ONES_ROWS, t), BF16)

    m_s[...] = jnp.full(m_s.shape, NEG_BIG, F32)
    acc_s[...] = jnp.zeros_like(acc_s)

    def scores(j, buf):
        sl = pl.ds(pl.multiple_of(j * t, t), t)
        kt = k_ref[0, sl, :]
        for c in range(2):
            buf[c] = _dot(kt, qc[c])

    def consume(j, buf, bias, cj):
        vaug = jnp.concatenate([vt_ref[0, j], ones], axis=0)
        for c in range(2):
            sb = buf[c] + bias
            m_old = m_s[c]
            m_new = jnp.maximum(m_old, jnp.max(sb, axis=0, keepdims=True) + cj)
            alpha = jnp.exp2(m_old - m_new)
            p = jnp.exp2(sb - (m_new - cj))
            acc_s[c] = alpha * acc_s[c] + _dot(vaug, p.astype(BF16))
            m_s[c] = m_new

    def off_diag(j, buf):
        consume(j, buf, obias[...], slope2 * ((j - i) * t).astype(F32))

    scores(0, s_a)

    def pair(pp, carry):
        j = 2 * pp
        scores(j + 1, s_b)
        off_diag(j, s_a)
        scores(j + 2, s_a)
        off_diag(j + 1, s_b)
        return carry

    lax.fori_loop(0, i // 2, pair, 0)

    @pl.when(i % 2 == 0)
    def _():
        consume(i, s_a, dbias[...], jnp.float32(0.0))

    @pl.when(i % 2 == 1)
    def _():
        scores(i, s_b)
        off_diag(i - 1, s_a)
        consume(i, s_b, dbias[...], jnp.float32(0.0))

    lam = (jnp.exp(jnp.sum(lq1_ref[...] * lk1_ref[...], axis=-1, keepdims=True))
           - jnp.exp(jnp.sum(lq2_ref[...] * lk2_ref[...], axis=-1, keepdims=True))
           + lambda_init)
    dv = DIFF_VDIM
    ot = (acc_s[0, 0:dv, :] / acc_s[0, dv:dv + 1, :]
          - lam * (acc_s[1, 0:dv, :] / acc_s[1, dv:dv + 1, :]))
    o = ot.T
    ms = jnp.mean(o * o, axis=-1, keepdims=True)
    o = o * lax.rsqrt(ms + EPS) * subln_ref[...] * (1.0 - lambda_init)
    o_ref[0] = o.astype(BF16)


def _diffattn(qt, k, vt, lq1, lk1, lq2, lk2, subln, lambda_init):
    B, S, _ = k.shape
    t = ATTN_T
    hw = 2 * DIFF_QK
    vec = pl.BlockSpec((1, DIFF_QK), lambda b, h, i: (0, 0))
    return pl.pallas_call(
        functools.partial(_diff_kernel, lambda_init=lambda_init),
        grid=(B, DIFF_HEADS, S // t),
        in_specs=[pl.BlockSpec((1, 1, hw, t), lambda b, h, i: (b, i, h, 0)),
                  pl.BlockSpec((1, S, hw), lambda b, h, i: (b, 0, h)),
                  pl.BlockSpec((1, S // t, DIFF_VDIM, t), lambda b, h, i: (b, 0, h, 0)),
                  vec, vec, vec, vec,
                  pl.BlockSpec((1, DIFF_VDIM), lambda b, h, i: (0, 0))],
        out_specs=pl.BlockSpec((1, t, DIFF_VDIM), lambda b, h, i: (b, i, h)),
        out_shape=jax.ShapeDtypeStruct((B, S, DIFF_WIDTH), BF16),
        scratch_shapes=[pltpu.VMEM((t, t), F32),
                        pltpu.VMEM((t, t), F32),
                        pltpu.VMEM((2, 1, t), F32),
                        pltpu.VMEM((2, DIFF_VDIM + DIFF_ONES_ROWS, t), F32),
                        pltpu.VMEM((2, t, t), F32),
                        pltpu.VMEM((2, t, t), F32)],
        compiler_params=_cparams(("parallel", "arbitrary", "arbitrary")),
        name="diffattn",
    )(qt, k, vt, lq1, lk1, lq2, lk2, subln)


def _memkv_kernel(mem_ref, g_ref, wk_ref, wv_ref, kn_ref, k_ref, v_ref):
    x = mem_ref[0]
    ms = jnp.mean(x * x, axis=-1, keepdims=True)
    xn = (x * lax.rsqrt(ms + EPS) * g_ref[...]).astype(BF16)
    k = _dot(xn, wk_ref[...])
    v = _dot(xn, wv_ref[...])
    ks = []
    for h in range(XA_HEADS):
        kh = k[:, h * XA_HEAD:(h + 1) * XA_HEAD]
        kms = jnp.mean(kh * kh, axis=-1, keepdims=True)
        ks.append(kh * lax.rsqrt(kms + EPS) * kn_ref[...])
    k_ref[0] = jnp.concatenate(ks, axis=1).astype(BF16)
    v_ref[0] = v.astype(BF16)


def _memkv(mem, g, wk_bf, wv_bf, k_norm):
    B, M, D = mem.shape
    const = lambda shape: pl.BlockSpec(shape, lambda b: (0,) * len(shape))
    tile = pl.BlockSpec((1, M, D), lambda b: (b, 0, 0))
    return pl.pallas_call(
        _memkv_kernel,
        grid=(B,),
        in_specs=[tile, const((1, D)), const((D, D)), const((D, D)), const((1, XA_HEAD))],
        out_specs=[tile, tile],
        out_shape=[jax.ShapeDtypeStruct((B, M, D), BF16), jax.ShapeDtypeStruct((B, M, D), BF16)],
        compiler_params=_cparams(("parallel",)),
        name="memkv",
    )(mem, g, wk_bf, wv_bf, k_norm)


def _mid_kernel(h_ref, yp_ref, yr_ref, yd_ref, wout_ref, g_ref, wq_ref, km_ref, vm_ref,
                wo_ref, qn_ref, o_ref):
    a = POOL_WIDTH
    b = POOL_WIDTH + RWKV_WIDTH
    h1 = (h_ref[0] + _dot(yp_ref[0], wout_ref[0:a, :]) + _dot(yr_ref[0], wout_ref[a:b, :])
          + _dot(yd_ref[0], wout_ref[b:D_MODEL, :]))
    ms = jnp.mean(h1 * h1, axis=-1, keepdims=True)
    xn = (h1 * lax.rsqrt(ms + EPS) * g_ref[...]).astype(BF16)
    q = _dot(xn, wq_ref[...])
    qscale = (XA_HEAD ** -0.5) * LOG2E
    outs = []
    for hd in range(XA_HEADS):
        sl = slice(hd * XA_HEAD, (hd + 1) * XA_HEAD)
        qh = q[:, sl]
        qms = jnp.mean(qh * qh, axis=-1, keepdims=True)
        qh = (qh * lax.rsqrt(qms + EPS) * (qn_ref[...] * qscale)).astype(BF16)
        s = _dot_nt(qh, km_ref[0, :, sl])
        m = jnp.max(s, axis=-1, keepdims=True)
        p = jnp.exp2(s - m)
        l = jnp.sum(p, axis=-1, keepdims=True)
        outs.append(_dot(p.astype(BF16), vm_ref[0, :, sl]) / l)
    o = jnp.concatenate(outs, axis=1).astype(BF16)
    o_ref[0] = h1 + _dot(o, wo_ref[...])


def _mid(h, yp, yr, yd, wout_bf, g, wq_bf, kmem, vmem, wo_bf, q_norm, tm):
    B, S, D = h.shape
    M = kmem.shape[1]
    const = lambda shape: pl.BlockSpec(shape, lambda b, i: (0,) * len(shape))
    tile = lambda c: pl.BlockSpec((1, tm, c), lambda b, i: (b, i, 0))
    memspec = pl.BlockSpec((1, M, D), lambda b, i: (b, 0, 0))
    return pl.pallas_call(
        _mid_kernel,
        grid=(B, S // tm),
        in_specs=[tile(D), tile(POOL_WIDTH), tile(RWKV_WIDTH), tile(DIFF_WIDTH), const((D, D)),
                  const((1, D)), const((D, D)), memspec, memspec, const((D, D)),
                  const((1, XA_HEAD))],
        out_specs=tile(D),
        out_shape=jax.ShapeDtypeStruct((B, S, D), F32),
        compiler_params=_cparams(("parallel", "parallel")),
        name="mid",
    )(h, yp, yr, yd, wout_bf, g, wq_bf, kmem, vmem, wo_bf, q_norm)


def _ffn_kernel(h_ref, g_ref, wa_ref, wb_ref, cw_ref, cb_ref, wd_ref, o_ref,
                xn_s, abuf, carry, *, tm):
    i = pl.program_id(1)
    j = pl.program_id(2)
    pad = V7X_SUBLANES

    @pl.when(j == 0)
    def _():
        x = h_ref[0]
        ms = jnp.mean(x * x, axis=-1, keepdims=True)
        xn_s[...] = (x * lax.rsqrt(ms + EPS) * g_ref[...]).astype(BF16)

    xn = xn_s[...]
    a = _dot(xn, wa_ref[...])
    b = _dot(xn, wb_ref[...])
    abuf[0:pad, :] = jnp.where(i == 0, 0.0, carry[j])
    abuf[pad:pad + tm, :] = a
    carry[j] = a[tm - pad:tm, :]
    cw = cw_ref[...]
    c = (cw[2:3, :] * a + cw[1:2, :] * abuf[pad - 1:pad - 1 + tm, :]
         + cw[0:1, :] * abuf[pad - 2:pad - 2 + tm, :] + cb_ref[...])
    gelu = 0.5 * c * (1.0 + lax.erf(c * (2.0 ** -0.5)))
    hmid = (gelu * b).astype(BF16)
    contrib = _dot(hmid, wd_ref[...])

    @pl.when(j == 0)
    def _():
        o_ref[0] = h_ref[0] + contrib

    @pl.when(j != 0)
    def _():
        o_ref[0] = o_ref[0] + contrib


def _ffn(h, g, wup_bf, conv_w, conv_b, wdown_bf, tm, nf):
    B, S, D = h.shape
    tf = D_FF // nf
    return pl.pallas_call(
        functools.partial(_ffn_kernel, tm=tm),
        grid=(B, S // tm, nf),
        in_specs=[pl.BlockSpec((1, tm, D), lambda b, i, j: (b, i, 0)),
                  pl.BlockSpec((1, D), lambda b, i, j: (0, 0)),
                  pl.BlockSpec((D, tf), lambda b, i, j: (0, j)),
                  pl.BlockSpec((D, tf), lambda b, i, j: (0, j + nf)),
                  pl.BlockSpec((CONV_W, tf), lambda b, i, j: (0, j)),
                  pl.BlockSpec((1, tf), lambda b, i, j: (0, j)),
                  pl.BlockSpec((tf, D), lambda b, i, j: (j, 0))],
        out_specs=pl.BlockSpec((1, tm, D), lambda b, i, j: (b, i, 0)),
        out_shape=jax.ShapeDtypeStruct((B, S, D), F32),
        scratch_shapes=[pltpu.VMEM((tm, D), BF16),
                        pltpu.VMEM((V7X_SUBLANES + tm, tf), F32),
                        pltpu.VMEM((nf, V7X_SUBLANES, tf), F32)],
        compiler_params=_cparams(("parallel", "arbitrary", "arbitrary")),
        name="ffn",
    )(h, g, wup_bf, wup_bf, conv_w, conv_b, wdown_bf)


def _tiles(S):
    pick = lambda pref: max(c for c in (64, 128, 256, 512, 1024) if c <= pref and S % c == 0)
    assert S % ATTN_T == 0
    return dict(mix=pick(512), pool=pick(1024), rwkv=pick(256), mid=pick(512),
                ffn=pick(512))


def _block_diag(blocks):
    n = len(blocks)
    rows = []
    for i, blk in enumerate(blocks):
        rows.append(jnp.concatenate(
            [blk if j == i else jnp.zeros_like(blk) for j in range(n)], axis=1))
    return jnp.concatenate(rows, axis=0)


def kernel(x, mem, mix_norm_g, w_in, pool_w, pool_scale, rwkv_mu, rwkv_w0, rwkv_w2, rwkv_a0, rwkv_a2, rwkv_g2, rwkv_k_k, rwkv_k_a, rwkv_r_k, rwkv_ln_w, rwkv_ln_b, diff_q_norm, diff_k_norm, diff_lq1, diff_lk1, diff_lq2, diff_lk2, diff_subln, w_out, xa_norm_g, mem_norm_g, xa_wq, xa_wk, xa_wv, xa_wo, xa_q_norm, xa_k_norm, ffn_norm_g, ffn_w_up, ffn_conv_w, ffn_conv_b, ffn_w_down):
    B, S, D = x.shape
    depth = w_in.shape[0]
    tl = _tiles(S)
    row = lambda a: a.reshape(1, -1).astype(F32)
    h = x
    for l in range(depth):
        lambda_init = 0.8 - 0.6 * math.exp(-0.3 * l)
        qgain = row(jnp.tile(diff_q_norm[l].reshape(-1), DIFF_HEADS)) * (DIFF_QK ** -0.5 * LOG2E)
        kgain = row(jnp.tile(diff_k_norm[l].reshape(-1), DIFF_HEADS))
        zeros64 = jnp.zeros((64, RWKV_WIDTH), F32)
        w2p = jnp.concatenate([rwkv_w2[l], zeros64], axis=0).astype(BF16)
        a2p = jnp.concatenate([zeros64, rwkv_a2[l]], axis=0).astype(BF16)
        pool_bd = _block_diag([pool_w[l, gi] for gi in range(len(POOL_WINDOWS))]).astype(BF16)

        z_pool, z_rwkv, qd, kd, vd = _mix_in(h, row(mix_norm_g[l]), w_in[l].astype(BF16),
                                             qgain, kgain, tl["mix"])
        y_pool = _pool(z_pool, pool_bd, row(pool_scale[l]), tl["pool"])
        y_rwkv = _rwkv(z_rwkv, row(rwkv_mu[l]), row(rwkv_w0[l]), w2p, row(rwkv_a0[l]), a2p,
                       rwkv_g2[l].astype(BF16), row(rwkv_k_k[l]), row(rwkv_k_a[l]),
                       row(rwkv_r_k[l]), row(rwkv_ln_w[l]), row(rwkv_ln_b[l]), tl["rwkv"])
        y_diff = _diffattn(qd, kd, vd, row(diff_lq1[l]), row(diff_lk1[l]), row(diff_lq2[l]),
                           row(diff_lk2[l]), row(diff_subln[l]), lambda_init)
        kmem, vmem = _memkv(mem, row(mem_norm_g[l]), xa_wk[l].astype(BF16),
                            xa_wv[l].astype(BF16), row(xa_k_norm[l]))
        h = _mid(h, y_pool, y_rwkv, y_diff, w_out[l].astype(BF16), row(xa_norm_g[l]),
                 xa_wq[l].astype(BF16), kmem, vmem, xa_wo[l].astype(BF16), row(xa_q_norm[l]),
                 tl["mid"])
        h = _ffn(h, row(ffn_norm_g[l]), ffn_w_up[l].astype(BF16), ffn_conv_w[l].astype(F32),
                 row(ffn_conv_b[l]), ffn_w_down[l].astype(BF16), tl["ffn"], 2)
    return h
```

```python
import functools
import math

import jax
import jax.numpy as jnp
import numpy as np
from jax import lax
from jax.experimental import pallas as pl
from jax.experimental.pallas import tpu as pltpu

F32 = jnp.float32
BF16 = jnp.bfloat16

D_MODEL = 1024
EPS = 1e-6
CHUNK = 64

POOL_WIDTH = 256
POOL_GDIM = 64
POOL_WINDOWS = (2, 4, 8, 16)
POOL_HALO = 16

RWKV_WIDTH = 256
RWKV_HEAD = 64
RWKV_COLS = 1024
RWKV_GN_EPS = 64e-5
RWKV_CHUNK = 64

DIFF_WIDTH = 512
DIFF_HEADS = 4
DIFF_VDIM = 128
DIFF_QK = 64
P_IN = POOL_WIDTH + RWKV_COLS + 3 * DIFF_WIDTH

XA_HEADS = 4
XA_HEAD = 256
D_FF = 2816
CONV_W = 3

LOG2E = math.log2(math.e)
NEG_BIG = -1e30
ATTN_T = 512

V7X_SUBLANES = 8
V7X_VMEM_LIMIT = 52 * 1024 * 1024


def _cparams(sem):
    return pltpu.CompilerParams(dimension_semantics=sem, vmem_limit_bytes=V7X_VMEM_LIMIT)


def _dot(a, b):
    return jnp.dot(a, b, preferred_element_type=F32)


def _dot_nt(a, b):
    return lax.dot_general(a, b, (((1,), (1,)), ((), ())), preferred_element_type=F32)


def _dot_tn(a, b):
    return lax.dot_general(a, b, (((0,), (0,)), ((), ())), preferred_element_type=F32)


def _group_ones(width, group):
    idx = np.arange(width) // group
    return jnp.asarray((idx[:, None] == idx[None, :]).astype(np.float32), dtype=BF16)


def _gsum1(x, ones_bd):
    w = ones_bd.shape[0]
    parts = [_dot(x[:, i:i + w].astype(BF16), ones_bd) for i in range(0, x.shape[1], w)]
    return parts[0] if len(parts) == 1 else jnp.concatenate(parts, axis=1)


def _gsum2(x, ones_bd):
    hi = x.astype(BF16)
    lo = (x - hi.astype(F32)).astype(BF16)
    return _dot(hi, ones_bd) + _dot(lo, ones_bd)


def _mix_in_kernel(x_ref, g_ref, w_ref, ones_ref, qg_ref, kg_ref,
                   zp_ref, zr_ref, q_ref, k_ref, v_ref, *, tm):
    x = x_ref[0]
    ms = jnp.mean(x * x, axis=-1, keepdims=True)
    xn = (x * lax.rsqrt(ms + EPS) * g_ref[...]).astype(BF16)
    z = _dot(xn, w_ref[...])
    zp_ref[0] = z[:, :POOL_WIDTH]
    zr_ref[0] = z[:, POOL_WIDTH:POOL_WIDTH + RWKV_COLS]
    o = POOL_WIDTH + RWKV_COLS
    q = z[:, o:o + DIFF_WIDTH]
    k = z[:, o + DIFF_WIDTH:o + 2 * DIFF_WIDTH]
    v = z[:, o + 2 * DIFF_WIDTH:o + 3 * DIFF_WIDTH]
    ones_bd = ones_ref[...]
    qss = _gsum1(q * q, ones_bd) * (1.0 / DIFF_QK)
    kss = _gsum1(k * k, ones_bd) * (1.0 / DIFF_QK)
    qn = q * lax.rsqrt(qss + EPS) * qg_ref[...]
    k_ref[0] = (k * lax.rsqrt(kss + EPS) * kg_ref[...]).astype(BF16)
    for n in range(tm // ATTN_T):
        rows = slice(n * ATTN_T, (n + 1) * ATTN_T)
        q_ref[0, n] = qn[rows, :].T.astype(BF16)
        v_ref[0, n] = v[rows, :].T.astype(BF16)


def _mix_in(h, g, w_in_bf, qgain, kgain, tm):
    B, S, D = h.shape
    ones_bd = _group_ones(256, DIFF_QK)
    const = lambda shape: pl.BlockSpec(shape, lambda b, i: (0,) * len(shape))
    tile = lambda c: pl.BlockSpec((1, tm, c), lambda b, i: (b, i, 0))
    tile_t = pl.BlockSpec((1, tm // ATTN_T, DIFF_WIDTH, ATTN_T), lambda b, i: (b, i, 0, 0))
    shape_t = jax.ShapeDtypeStruct((B, S // ATTN_T, DIFF_WIDTH, ATTN_T), BF16)
    return pl.pallas_call(
        functools.partial(_mix_in_kernel, tm=tm),
        grid=(B, S // tm),
        in_specs=[tile(D), const((1, D)), const((D, P_IN)), const((256, 256)),
                  const((1, DIFF_WIDTH)), const((1, DIFF_WIDTH))],
        out_specs=[tile(POOL_WIDTH), tile(RWKV_COLS), tile_t, tile(DIFF_WIDTH), tile_t],
        out_shape=[jax.ShapeDtypeStruct((B, S, POOL_WIDTH), F32),
                   jax.ShapeDtypeStruct((B, S, RWKV_COLS), F32),
                   shape_t,
                   jax.ShapeDtypeStruct((B, S, DIFF_WIDTH), BF16),
                   shape_t],
        compiler_params=_cparams(("parallel", "parallel")),
        name="mix_in",
    )(h, g, w_in_bf, ones_bd, qgain, kgain)


def _pool_kernel(z_ref, halo_ref, w_ref, scale_ref, y_ref, buf_ref, *, tm):
    i = pl.program_id(1)
    u = z_ref[0]
    pad = V7X_SUBLANES
    n = tm + POOL_HALO
    halo = jnp.where(i == 0, 0.0, halo_ref[0])
    buf_ref[0:pad, :] = jnp.zeros((pad, POOL_WIDTH), F32)
    buf_ref[pad:pad + POOL_HALO, :] = halo
    buf_ref[pad + POOL_HALO:pad + n, :] = u
    lane = lax.broadcasted_iota(jnp.int32, (tm, POOL_WIDTH), 1)
    grp = lane // POOL_GDIM
    win = jnp.zeros((tm, POOL_WIDTH), F32)
    shift = 1
    for gi, w in enumerate(POOL_WINDOWS):
        while shift < w:
            cur = buf_ref[pad:pad + n, :] + buf_ref[pad - shift:pad - shift + n, :]
            buf_ref[pad:pad + n, :] = cur
            shift *= 2
        win = jnp.where(grp == gi, buf_ref[pad + POOL_HALO:pad + n, :], win)
    t = i * tm + lax.broadcasted_iota(jnp.int32, (tm, POOL_WIDTH), 0)
    wlane = jnp.left_shift(2, grp)
    count = jnp.minimum(t + 1, wlane).astype(F32)
    d = win / count - u
    y = _dot(d.astype(BF16), w_ref[...]) * scale_ref[...]
    y_ref[0] = y.astype(BF16)


def _pool(z_pool, w_bd_bf, scale, tm):
    B, S, _ = z_pool.shape
    r = tm // POOL_HALO
    return pl.pallas_call(
        functools.partial(_pool_kernel, tm=tm),
        grid=(B, S // tm),
        in_specs=[pl.BlockSpec((1, tm, POOL_WIDTH), lambda b, i: (b, i, 0)),
                  pl.BlockSpec((1, POOL_HALO, POOL_WIDTH),
                               lambda b, i: (b, jnp.maximum(i * r - 1, 0), 0)),
                  pl.BlockSpec((POOL_WIDTH, POOL_WIDTH), lambda b, i: (0, 0)),
                  pl.BlockSpec((1, POOL_WIDTH), lambda b, i: (0, 0))],
        out_specs=pl.BlockSpec((1, tm, POOL_WIDTH), lambda b, i: (b, i, 0)),
        out_shape=jax.ShapeDtypeStruct((B, S, POOL_WIDTH), BF16),
        scratch_shapes=[pltpu.VMEM((V7X_SUBLANES + POOL_HALO + tm, POOL_WIDTH), F32)],
        compiler_params=_cparams(("parallel", "parallel")),
        name="pool",
    )(z_pool, z_pool, w_bd_bf, scale)


RWKV_PAIR = 2 * RWKV_HEAD


def _bd2(x, lo_mask):
    x = x.astype(BF16)
    zero = jnp.zeros_like(x)
    return jnp.concatenate([jnp.where(lo_mask, x, zero), jnp.where(lo_mask, zero, x)], axis=0)


def _diag_blocks(full, lo_mask):
    n = full.shape[1] // RWKV_PAIR
    lo = jnp.concatenate([lo_mask] * n, axis=1) if n > 1 else lo_mask
    return jnp.where(lo, full[0:RWKV_HEAD], full[RWKV_HEAD:2 * RWKV_HEAD])


def _rwkv_kernel(z_ref, halo_ref, mu_ref, w0_ref, w2_ref, a0_ref, a2_ref, g2_ref,
                 kk_ref, ka_ref, rk_ref, lnw_ref, lnb_ref, ones_ref, tri_ref,
                 y_ref,
                 zbuf, r_s, k_s, v_s, lw_s, a_s, b_s, y_s, state, rh_s, g_s, ml_s, *, tm):
    i = pl.program_id(1)
    L = RWKV_CHUNK
    W = RWKV_WIDTH
    pad = V7X_SUBLANES

    @pl.when(i == 0)
    def _():
        state[...] = jnp.zeros_like(state)

    z = z_ref[0]
    zbuf[0:pad, :] = jnp.where(i == 0, 0.0, halo_ref[0])
    zbuf[pad:pad + tm, :] = z
    zprev = zbuf[pad - 1:pad - 1 + tm, :]
    zm = z + mu_ref[...] * (zprev - z)
    r = zm[:, 0:W]
    k = zm[:, W:2 * W]
    v = zm[:, 2 * W:3 * W]
    z6 = zm[:, 3 * W:3 * W + 128]
    gd = zm[:, 3 * W + 128:3 * W + 256]
    ones_bd = ones_ref[...]
    wl = w0_ref[...] + _dot(jnp.tanh(z6).astype(BF16), w2_ref[...])
    w = -jax.nn.softplus(-wl) - 0.5
    a = jax.nn.sigmoid(a0_ref[...] + _dot(z6.astype(BF16), a2_ref[...]))
    g = _dot(jax.nn.sigmoid(gd).astype(BF16), g2_ref[...])
    kk = k * kk_ref[...]
    kk = kk / jnp.maximum(jnp.sqrt(_gsum2(kk * kk, ones_bd)), 1e-12)
    kp = k * (1.0 + (a - 1.0) * ka_ref[...])
    r_s[...] = r
    k_s[...] = kp
    v_s[...] = v
    lw_s[...] = -jnp.exp(w)
    a_s[...] = -kk
    b_s[...] = kk * a

    PW = RWKV_PAIR
    n_pairs = W // PW
    lane2 = lax.broadcasted_iota(jnp.int32, (L, PW), 1)
    trow = lax.broadcasted_iota(jnp.int32, (L, PW), 0)
    lo_mask = lane2 < RWKV_HEAD
    jcol = jnp.bitwise_and(lane2, RWKV_HEAD - 1)
    strict = trow > jcol
    incl = trow >= jcol
    eye = trow == jcol
    bf = lambda x: x.astype(BF16)
    rows2 = lambda x, y: jnp.concatenate([x, y], axis=0)

    n_chunks = tm // L
    chains = [(ci, pg) for ci in range(n_chunks) for pg in range(n_pairs)]
    pre = []
    for ci in range(n_chunks):
        sl = slice(ci * L, (ci + 1) * L)
        lw = lw_s[sl, :]
        c_in = jnp.dot(tri_ref[...], lw, precision=lax.Precision.HIGHEST,
                       preferred_element_type=F32)
        c_tot = c_in[L - 1:L, :]
        e_neg = jnp.exp(-c_in)
        e_rem = jnp.exp(c_tot - c_in)
        pre.append(dict(
            at=a_s[sl, :] * jnp.exp(c_in - lw), rt=r_s[sl, :] * jnp.exp(c_in),
            bt=b_s[sl, :] * e_neg, kt=k_s[sl, :] * e_neg,
            bbar=b_s[sl, :] * e_rem, kbar=k_s[sl, :] * e_rem,
            v=v_s[sl, :], e_tot=jnp.exp(c_tot)))
    pair = lambda ci, pg, name: pre[ci][name][:, pg * PW:(pg + 1) * PW]

    st = []
    for ci, pg in chains:
        at, rt = pair(ci, pg, "at"), pair(ci, pg, "rt")
        lhs = bf(rows2(at, rt))
        tb = _dot_nt(lhs, _bd2(pair(ci, pg, "bt"), lo_mask))
        tk = _dot_nt(lhs, _bd2(pair(ci, pg, "kt"), lo_mask))
        tab = jnp.where(strict, tb[0:L], 0.0)
        st.append(dict(
            at=at, rt=rt, v=pair(ci, pg, "v"), tab=tab,
            trb=bf(jnp.where(incl, tb[L:2 * L], 0.0)),
            tak=jnp.where(strict, tk[0:L], 0.0), trk=jnp.where(incl, tk[L:2 * L], 0.0),
            winv=jnp.where(eye, 1.0, 0.0) + tab))
    for c in st:
        c["p"] = _dot(bf(c["tab"]), _bd2(c["tab"], lo_mask))
    for _ in range(int(math.log2(L)) - 2):
        for c in st:
            res = _dot(bf(rows2(c["p"], c["winv"])), _bd2(c["p"], lo_mask))
            c["p"] = res[0:L]
            c["winv"] = c["winv"] + res[L:2 * L]
    for c in st:
        c["winv"] = c["winv"] + _dot(bf(c["winv"]), _bd2(c["p"], lo_mask))
        c["tv"] = _dot(bf(rows2(c["tak"], c["trk"])), _bd2(c["v"], lo_mask))
    for c in st:
        x_bd = jnp.concatenate([_bd2(c["at"], lo_mask), _bd2(c["tv"][0:L], lo_mask)], axis=1)
        c["wx"] = _dot(bf(c["winv"]), x_bd)
    for c in st:
        wx = c["wx"]
        ax_bd = jnp.concatenate([_bd2(wx[:, 0:PW], lo_mask), _bd2(wx[:, PW:2 * PW], lo_mask)],
                                axis=1)
        zz = _dot(c["trb"], ax_bd)
        c["rh"] = c["rt"] + zz[:, 0:PW]
        c["yl"] = zz[:, PW:2 * PW] + c["tv"][L:2 * L]
    for (ci, pg), c in zip(chains, st):
        v = c["v"]
        lhs_t = bf(rows2(pair(ci, pg, "bbar"), pair(ci, pg, "kbar")))
        rhs_t = bf(rows2(c["wx"], jnp.concatenate([jnp.zeros_like(v), v], axis=1)))
        full = _diag_blocks(_dot_tn(lhs_t, rhs_t), lo_mask)
        gs = slice(pg * PW, (pg + 1) * PW)
        sl = slice(ci * L, (ci + 1) * L)
        rh_s[sl, gs] = bf(c["rh"])
        y_s[sl, gs] = c["yl"]
        g_s[ci, :, gs] = bf(full[:, 0:PW] + jnp.where(eye, pre[ci]["e_tot"][:, gs], 0.0))
        ml_s[ci, :, gs] = full[:, PW:2 * PW]

    m_cur = [state[:, g * PW:(g + 1) * PW] for g in range(n_pairs)]
    for ci in range(n_chunks):
        sl = slice(ci * L, (ci + 1) * L)
        for pg in range(n_pairs):
            gs = slice(pg * PW, (pg + 1) * PW)
            res = _dot(rows2(rh_s[sl, gs], g_s[ci, :, gs]), _bd2(m_cur[pg], lo_mask))
            y_s[sl, gs] = y_s[sl, gs] + res[0:L]
            m_cur[pg] = res[L:2 * L] + ml_s[ci, :, gs]
    for pg in range(n_pairs):
        state[:, pg * PW:(pg + 1) * PW] = m_cur[pg]

    y = y_s[...]
    r = r_s[...]
    kp = k_s[...]
    v = v_s[...]
    inv_n = 1.0 / RWKV_HEAD
    mean = _gsum2(y, ones_bd) * inv_n
    yc = y - mean
    var = _gsum2(yc * yc, ones_bd) * inv_n
    yn = yc * lax.rsqrt(var + RWKV_GN_EPS) * lnw_ref[...] + lnb_ref[...]
    bonus = _gsum2(r * kp * rk_ref[...], ones_bd) * v
    y_ref[0] = ((yn + bonus) * g).astype(BF16)


def _rwkv(z_rwkv, mu, w0, w2p, a0, a2p, g2, k_k, k_a, r_k, ln_w, ln_b, tm):
    B, S, C = z_rwkv.shape
    W = RWKV_WIDTH
    L = RWKV_CHUNK
    ones_bd = _group_ones(W, RWKV_HEAD)
    tri = jnp.asarray((np.arange(L)[:, None] >= np.arange(L)[None, :]).astype(np.float32))
    r = tm // V7X_SUBLANES
    const = lambda shape: pl.BlockSpec(shape, lambda b, i: (0,) * len(shape))
    vec = const((1, W))
    sq = const((W, W))
    return pl.pallas_call(
        functools.partial(_rwkv_kernel, tm=tm),
        grid=(B, S // tm),
        in_specs=[pl.BlockSpec((1, tm, C), lambda b, i: (b, i, 0)),
                  pl.BlockSpec((1, V7X_SUBLANES, C),
                               lambda b, i: (b, jnp.maximum(i * r - 1, 0), 0)),
                  const((1, C)), vec, const((128, W)), vec, const((128, W)), const((128, W)),
                  vec, vec, vec, vec, vec, sq, const((L, L))],
        out_specs=pl.BlockSpec((1, tm, W), lambda b, i: (b, i, 0)),
        out_shape=jax.ShapeDtypeStruct((B, S, W), BF16),
        scratch_shapes=[pltpu.VMEM((V7X_SUBLANES + tm, C), F32)]
        + [pltpu.VMEM((tm, W), F32) for _ in range(7)]
        + [pltpu.VMEM((RWKV_HEAD, W), F32),
           pltpu.VMEM((tm, W), BF16),
           pltpu.VMEM((tm // L, RWKV_HEAD, W), BF16),
           pltpu.VMEM((tm // L, RWKV_HEAD, W), F32)],
        compiler_params=_cparams(("parallel", "arbitrary")),
        name="rwkv",
    )(z_rwkv, z_rwkv, mu, w0, w2p, a0, a2p, g2, k_k, k_a, r_k, ln_w, ln_b, ones_bd, tri)


DIFF_ONES_ROWS = 16


def _diff_kernel(qt_ref, k_ref, vt_ref, lq1_ref, lk1_ref, lq2_ref, lk2_ref, subln_ref,
                 o_ref, bias2, m_s, acc_s, s_a, s_b, mx_a, mx_b, *, lambda_init):
    t = ATTN_T
    h = pl.program_id(1)
    i = pl.program_id(2)
    slope2 = jnp.float32(0.0)
    for hh in range(DIFF_HEADS):
        slope2 = jnp.where(h == hh, 2.0 ** (-8.0 * (hh + 1) / DIFF_HEADS) * LOG2E, slope2)

    @pl.when(i == 0)
    def _():
        kc = lax.broadcasted_iota(jnp.int32, (t, t), 0)
        qr = lax.broadcasted_iota(jnp.int32, (t, t), 1)
        vis = (kc // CHUNK) <= (qr // CHUNK)
        rel = (qr - jnp.abs(qr - kc)).astype(F32)
        bias2[0] = slope2 * kc.astype(F32)
        bias2[1] = jnp.where(vis, slope2 * rel, NEG_BIG)

    qt = qt_ref[0, 0]
    row = lax.broadcasted_iota(jnp.int32, (2 * DIFF_QK, t), 0)
    qc = [jnp.where(row < DIFF_QK, qt, jnp.zeros_like(qt)),
          jnp.where(row >= DIFF_QK, qt, jnp.zeros_like(qt))]
    ones = jnp.ones((DIFF_ONES_ROWS, t), BF16)

    m_s[...] = jnp.full(m_s.shape, NEG_BIG, F32)
    acc_s[...] = jnp.zeros_like(acc_s)

    def scores(j, buf, mx):
        sl = pl.ds(pl.multiple_of(j * t, t), t)
        kt = k_ref[0, sl, :]
        bias = bias2[(j == i).astype(jnp.int32)]
        for c in range(2):
            sb = _dot(kt, qc[c]) + bias
            buf[c] = sb
            mx[c] = jnp.max(sb, axis=0, keepdims=True)

    def consume(j, buf, mx):
        cj = slope2 * ((j - i) * t).astype(F32)
        vaug = jnp.concatenate([vt_ref[0, j], ones], axis=0)
        for c in range(2):
            m_old = m_s[c]
            m_new = jnp.maximum(m_old, mx[c] + cj)
            alpha = jnp.exp2(m_old - m_new)
            p = jnp.exp2(buf[c] - (m_new - cj))
            acc_s[c] = alpha * acc_s[c] + _dot(vaug, p.astype(BF16))
            m_s[c] = m_new

    scores(0, s_a, mx_a)

    def pair(pp, carry):
        j = 2 * pp
        scores(j + 1, s_b, mx_b)
        consume(j, s_a, mx_a)
        scores(j + 2, s_a, mx_a)
        consume(j + 1, s_b, mx_b)
        return carry

    lax.fori_loop(0, i // 2, pair, 0)

    @pl.when(i % 2 == 0)
    def _():
        consume(i, s_a, mx_a)

    @pl.when(i % 2 == 1)
    def _():
        scores(i, s_b, mx_b)
        consume(i - 1, s_a, mx_a)
        consume(i, s_b, mx_b)

    lam = (jnp.exp(jnp.sum(lq1_ref[...] * lk1_ref[...], axis=-1, keepdims=True))
           - jnp.exp(jnp.sum(lq2_ref[...] * lk2_ref[...], axis=-1, keepdims=True))
           + lambda_init)
    dv = DIFF_VDIM
    ot = (acc_s[0, 0:dv, :] / acc_s[0, dv:dv + 1, :]
          - lam * (acc_s[1, 0:dv, :] / acc_s[1, dv:dv + 1, :]))
    o = ot.T
    ms = jnp.mean(o * o, axis=-1, keepdims=True)
    o = o * lax.rsqrt(ms + EPS) * subln_ref[...] * (1.0 - lambda_init)
    o_ref[0] = o.astype(BF16)


def _diffattn(qt, k, vt, lq1, lk1, lq2, lk2, subln, lambda_init):
    B, S, _ = k.shape
    t = ATTN_T
    hw = 2 * DIFF_QK
    vec = pl.BlockSpec((1, DIFF_QK), lambda b, h, i: (0, 0))
    return pl.pallas_call(
        functools.partial(_diff_kernel, lambda_init=lambda_init),
        grid=(B, DIFF_HEADS, S // t),
        in_specs=[pl.BlockSpec((1, 1, hw, t), lambda b, h, i: (b, i, h, 0)),
                  pl.BlockSpec((1, S, hw), lambda b, h, i: (b, 0, h)),
                  pl.BlockSpec((1, S // t, DIFF_VDIM, t), lambda b, h, i: (b, 0, h, 0)),
                  vec, vec, vec, vec,
                  pl.BlockSpec((1, DIFF_VDIM), lambda b, h, i: (0, 0))],
        out_specs=pl.BlockSpec((1, t, DIFF_VDIM), lambda b, h, i: (b, i, h)),
        out_shape=jax.ShapeDtypeStruct((B, S, DIFF_WIDTH), BF16),
        scratch_shapes=[pltpu.VMEM((2, t, t), F32),
                        pltpu.VMEM((2, 1, t), F32),
                        pltpu.VMEM((2, DIFF_VDIM + DIFF_ONES_ROWS, t), F32),
                        pltpu.VMEM((2, t, t), F32),
                        pltpu.VMEM((2, t, t), F32),
                        pltpu.VMEM((2, 1, t), F32),
                        pltpu.VMEM((2, 1, t), F32)],
        compiler_params=_cparams(("parallel", "arbitrary", "arbitrary")),
        name="diffattn",
    )(qt, k, vt, lq1, lk1, lq2, lk2, subln)


def _memkv_kernel(mem_ref, g_ref, wk_ref, wv_ref, kn_ref, k_ref, v_ref):
    x = mem_ref[0]
    ms = jnp.mean(x * x, axis=-1, keepdims=True)
    xn = (x * lax.rsqrt(ms + EPS) * g_ref[...]).astype(BF16)
    k = _dot(xn, wk_ref[...])
    v = _dot(xn, wv_ref[...])
    ks = []
    for h in range(XA_HEADS):
        kh = k[:, h * XA_HEAD:(h + 1) * XA_HEAD]
        kms = jnp.mean(kh * kh, axis=-1, keepdims=True)
        ks.append(kh * lax.rsqrt(kms + EPS) * kn_ref[...])
    k_ref[0] = jnp.concatenate(ks, axis=1).astype(BF16)
    v_ref[0] = v.astype(BF16)


def _memkv(mem, g, wk_bf, wv_bf, k_norm):
    B, M, D = mem.shape
    const = lambda shape: pl.BlockSpec(shape, lambda b: (0,) * len(shape))
    tile = pl.BlockSpec((1, M, D), lambda b: (b, 0, 0))
    return pl.pallas_call(
        _memkv_kernel,
        grid=(B,),
        in_specs=[tile, const((1, D)), const((D, D)), const((D, D)), const((1, XA_HEAD))],
        out_specs=[tile, tile],
        out_shape=[jax.ShapeDtypeStruct((B, M, D), BF16), jax.ShapeDtypeStruct((B, M, D), BF16)],
        compiler_params=_cparams(("parallel",)),
        name="memkv",
    )(mem, g, wk_bf, wv_bf, k_norm)


def _mid_kernel(h_ref, yp_ref, yr_ref, yd_ref, wout_ref, g_ref, wq_ref, km_ref, vm_ref,
                wo_ref, qn_ref, o_ref):
    a = POOL_WIDTH
    b = POOL_WIDTH + RWKV_WIDTH
    h1 = (h_ref[0] + _dot(yp_ref[0], wout_ref[0:a, :]) + _dot(yr_ref[0], wout_ref[a:b, :])
          + _dot(yd_ref[0], wout_ref[b:D_MODEL, :]))
    ms = jnp.mean(h1 * h1, axis=-1, keepdims=True)
    xn = (h1 * lax.rsqrt(ms + EPS) * g_ref[...]).astype(BF16)
    q = _dot(xn, wq_ref[...])
    qscale = (XA_HEAD ** -0.5) * LOG2E
    outs = []
    for hd in range(XA_HEADS):
        sl = slice(hd * XA_HEAD, (hd + 1) * XA_HEAD)
        qh = q[:, sl]
        qms = jnp.mean(qh * qh, axis=-1, keepdims=True)
        qh = (qh * lax.rsqrt(qms + EPS) * (qn_ref[...] * qscale)).astype(BF16)
        s = _dot_nt(qh, km_ref[0, :, sl])
        m = jnp.max(s, axis=-1, keepdims=True)
        p = jnp.exp2(s - m)
        l = jnp.sum(p, axis=-1, keepdims=True)
        outs.append(_dot(p.astype(BF16), vm_ref[0, :, sl]) / l)
    o = jnp.concatenate(outs, axis=1).astype(BF16)
    o_ref[0] = h1 + _dot(o, wo_ref[...])


def _mid(h, yp, yr, yd, wout_bf, g, wq_bf, kmem, vmem, wo_bf, q_norm, tm):
    B, S, D = h.shape
    M = kmem.shape[1]
    const = lambda shape: pl.BlockSpec(shape, lambda b, i: (0,) * len(shape))
    tile = lambda c: pl.BlockSpec((1, tm, c), lambda b, i: (b, i, 0))
    memspec = pl.BlockSpec((1, M, D), lambda b, i: (b, 0, 0))
    return pl.pallas_call(
        _mid_kernel,
        grid=(B, S // tm),
        in_specs=[tile(D), tile(POOL_WIDTH), tile(RWKV_WIDTH), tile(DIFF_WIDTH), const((D, D)),
                  const((1, D)), const((D, D)), memspec, memspec, const((D, D)),
                  const((1, XA_HEAD))],
        out_specs=tile(D),
        out_shape=jax.ShapeDtypeStruct((B, S, D), F32),
        compiler_params=_cparams(("parallel", "parallel")),
        name="mid",
    )(h, yp, yr, yd, wout_bf, g, wq_bf, kmem, vmem, wo_bf, q_norm)


def _ffn_kernel(h_ref, g_ref, wa_ref, wb_ref, cw_ref, cb_ref, wd_ref, o_ref,
                xn_s, abuf, carry, *, tm):
    i = pl.program_id(1)
    j = pl.program_id(2)
    pad = V7X_SUBLANES

    @pl.when(j == 0)
    def _():
        x = h_ref[0]
        ms = jnp.mean(x * x, axis=-1, keepdims=True)
        xn_s[...] = (x * lax.rsqrt(ms + EPS) * g_ref[...]).astype(BF16)

    xn = xn_s[...]
    a = _dot(xn, wa_ref[...])
    b = _dot(xn, wb_ref[...])
    abuf[0:pad, :] = jnp.where(i == 0, 0.0, carry[j])
    abuf[pad:pad + tm, :] = a
    carry[j] = a[tm - pad:tm, :]
    cw = cw_ref[...]
    c = (cw[2:3, :] * a + cw[1:2, :] * abuf[pad - 1:pad - 1 + tm, :]
         + cw[0:1, :] * abuf[pad - 2:pad - 2 + tm, :] + cb_ref[...])
    gelu = 0.5 * c * (1.0 + lax.erf(c * (2.0 ** -0.5)))
    hmid = (gelu * b).astype(BF16)
    contrib = _dot(hmid, wd_ref[...])

    @pl.when(j == 0)
    def _():
        o_ref[0] = h_ref[0] + contrib

    @pl.when(j != 0)
    def _():
        o_ref[0] = o_ref[0] + contrib


def _ffn(h, g, wup_bf, conv_w, conv_b, wdown_bf, tm, nf):
    B, S, D = h.shape
    tf = D_FF // nf
    return pl.pallas_call(
        functools.partial(_ffn_kernel, tm=tm),
        grid=(B, S // tm, nf),
        in_specs=[pl.BlockSpec((1, tm, D), lambda b, i, j: (b, i, 0)),
                  pl.BlockSpec((1, D), lambda b, i, j: (0, 0)),
                  pl.BlockSpec((D, tf), lambda b, i, j: (0, j)),
                  pl.BlockSpec((D, tf), lambda b, i, j: (0, j + nf)),
                  pl.BlockSpec((CONV_W, tf), lambda b, i, j: (0, j)),
                  pl.BlockSpec((1, tf), lambda b, i, j: (0, j)),
                  pl.BlockSpec((tf, D), lambda b, i, j: (j, 0))],
        out_specs=pl.BlockSpec((1, tm, D), lambda b, i, j: (b, i, 0)),
        out_shape=jax.ShapeDtypeStruct((B, S, D), F32),
        scratch_shapes=[pltpu.VMEM((tm, D), BF16),
                        pltpu.VMEM((V7X_SUBLANES + tm, tf), F32),
                        pltpu.VMEM((nf, V7X_SUBLANES, tf), F32)],
        compiler_params=_cparams(("parallel", "arbitrary", "arbitrary")),
        name="ffn",
    )(h, g, wup_bf, wup_bf, conv_w, conv_b, wdown_bf)


def _tiles(S):
    pick = lambda pref: max(c for c in (64, 128, 256, 512, 1024) if c <= pref and S % c == 0)
    assert S % ATTN_T == 0
    return dict(mix=pick(512), pool=pick(1024), rwkv=pick(256), mid=pick(512),
                ffn=pick(512))


def _block_diag(blocks):
    n = len(blocks)
    rows = []
    for i, blk in enumerate(blocks):
        rows.append(jnp.concatenate(
            [blk if j == i else jnp.zeros_like(blk) for j in range(n)], axis=1))
    return jnp.concatenate(rows, axis=0)


def kernel(x, mem, mix_norm_g, w_in, pool_w, pool_scale, rwkv_mu, rwkv_w0, rwkv_w2, rwkv_a0, rwkv_a2, rwkv_g2, rwkv_k_k, rwkv_k_a, rwkv_r_k, rwkv_ln_w, rwkv_ln_b, diff_q_norm, diff_k_norm, diff_lq1, diff_lk1, diff_lq2, diff_lk2, diff_subln, w_out, xa_norm_g, mem_norm_g, xa_wq, xa_wk, xa_wv, xa_wo, xa_q_norm, xa_k_norm, ffn_norm_g, ffn_w_up, ffn_conv_w, ffn_conv_b, ffn_w_down):
    B, S, D = x.shape
    depth = w_in.shape[0]
    tl = _tiles(S)
    row = lambda a: a.reshape(1, -1).astype(F32)
    h = x
    for l in range(depth):
        lambda_init = 0.8 - 0.6 * math.exp(-0.3 * l)
        qgain = row(jnp.tile(diff_q_norm[l].reshape(-1), DIFF_HEADS)) * (DIFF_QK ** -0.5 * LOG2E)
        kgain = row(jnp.tile(diff_k_norm[l].reshape(-1), DIFF_HEADS))
        zeros64 = jnp.zeros((64, RWKV_WIDTH), F32)
        w2p = jnp.concatenate([rwkv_w2[l], zeros64], axis=0).astype(BF16)
        a2p = jnp.concatenate([zeros64, rwkv_a2[l]], axis=0).astype(BF16)
        pool_bd = _block_diag([pool_w[l, gi] for gi in range(len(POOL_WINDOWS))]).astype(BF16)

        z_pool, z_rwkv, qd, kd, vd = _mix_in(h, row(mix_norm_g[l]), w_in[l].astype(BF16),
                                             qgain, kgain, tl["mix"])
        y_pool = _pool(z_pool, pool_bd, row(pool_scale[l]), tl["pool"])
        y_rwkv = _rwkv(z_rwkv, row(rwkv_mu[l]), row(rwkv_w0[l]), w2p, row(rwkv_a0[l]), a2p,
                       rwkv_g2[l].astype(BF16), row(rwkv_k_k[l]), row(rwkv_k_a[l]),
                       row(rwkv_r_k[l]), row(rwkv_ln_w[l]), row(rwkv_ln_b[l]), tl["rwkv"])
        y_diff = _diffattn(qd, kd, vd, row(diff_lq1[l]), row(diff_lk1[l]), row(diff_lq2[l]),
                           row(diff_lk2[l]), row(diff_subln[l]), lambda_init)
        kmem, vmem = _memkv(mem, row(mem_norm_g[l]), xa_wk[l].astype(BF16),
                            xa_wv[l].astype(BF16), row(xa_k_norm[l]))
        h = _mid(h, y_pool, y_rwkv, y_diff, w_out[l].astype(BF16), row(xa_norm_g[l]),
                 xa_wq[l].astype(BF16), kmem, vmem, xa_wo[l].astype(BF16), row(xa_q_norm[l]),
                 tl["mid"])
        h = _ffn(h, row(ffn_norm_g[l]), ffn_w_up[l].astype(BF16), ffn_conv_w[l].astype(F32),
                 row(ffn_conv_b[l]), ffn_w_down[l].astype(BF16), tl["ffn"], 2)
    return h
```

```python
import functools
import math

import jax
import jax.numpy as jnp
import numpy as np
from jax import lax
from jax.experimental import pallas as pl
from jax.experimental.pallas import tpu as pltpu

F32 = jnp.float32
BF16 = jnp.bfloat16

D_MODEL = 1024
EPS = 1e-6
CHUNK = 64

POOL_WIDTH = 256
POOL_GDIM = 64
POOL_WINDOWS = (2, 4, 8, 16)
POOL_HALO = 16

RWKV_WIDTH = 256
RWKV_HEAD = 64
RWKV_COLS = 1024
RWKV_GN_EPS = 64e-5
RWKV_CHUNK = 64

DIFF_WIDTH = 512
DIFF_HEADS = 4
DIFF_VDIM = 128
DIFF_QK = 64
P_IN = POOL_WIDTH + RWKV_COLS + 3 * DIFF_WIDTH

XA_HEADS = 4
XA_HEAD = 256
D_FF = 2816
CONV_W = 3

LOG2E = math.log2(math.e)
NEG_BIG = -1e30
ATTN_T = 512

V7X_SUBLANES = 8
V7X_VMEM_LIMIT = 52 * 1024 * 1024


def _cparams(sem):
    return pltpu.CompilerParams(dimension_semantics=sem, vmem_limit_bytes=V7X_VMEM_LIMIT)


def _dot(a, b):
    return jnp.dot(a, b, preferred_element_type=F32)


def _dot_nt(a, b):
    return lax.dot_general(a, b, (((1,), (1,)), ((), ())), preferred_element_type=F32)


def _dot_tn(a, b):
    return lax.dot_general(a, b, (((0,), (0,)), ((), ())), preferred_element_type=F32)


def _group_ones(width, group):
    idx = np.arange(width) // group
    return jnp.asarray((idx[:, None] == idx[None, :]).astype(np.float32), dtype=BF16)


def _gsum1(x, ones_bd):
    w = ones_bd.shape[0]
    parts = [_dot(x[:, i:i + w].astype(BF16), ones_bd) for i in range(0, x.shape[1], w)]
    return parts[0] if len(parts) == 1 else jnp.concatenate(parts, axis=1)


def _gsum2(x, ones_bd):
    hi = x.astype(BF16)
    lo = (x - hi.astype(F32)).astype(BF16)
    return _dot(hi, ones_bd) + _dot(lo, ones_bd)


def _mix_in_kernel(x_ref, g_ref, w_ref, ones_ref, qg_ref, kg_ref,
                   zp_ref, zr_ref, q_ref, k_ref, v_ref, *, tm):
    x = x_ref[0]
    ms = jnp.mean(x * x, axis=-1, keepdims=True)
    xn = (x * lax.rsqrt(ms + EPS) * g_ref[...]).astype(BF16)
    z = _dot(xn, w_ref[...])
    zp_ref[0] = z[:, :POOL_WIDTH]
    zr_ref[0] = z[:, POOL_WIDTH:POOL_WIDTH + RWKV_COLS]
    o = POOL_WIDTH + RWKV_COLS
    q = z[:, o:o + DIFF_WIDTH]
    k = z[:, o + DIFF_WIDTH:o + 2 * DIFF_WIDTH]
    v = z[:, o + 2 * DIFF_WIDTH:o + 3 * DIFF_WIDTH]
    ones_bd = ones_ref[...]
    qss = _gsum1(q * q, ones_bd) * (1.0 / DIFF_QK)
    kss = _gsum1(k * k, ones_bd) * (1.0 / DIFF_QK)
    qn = q * lax.rsqrt(qss + EPS) * qg_ref[...]
    k_ref[0] = (k * lax.rsqrt(kss + EPS) * kg_ref[...]).astype(BF16)
    for n in range(tm // ATTN_T):
        rows = slice(n * ATTN_T, (n + 1) * ATTN_T)
        q_ref[0, n] = qn[rows, :].T.astype(BF16)
        v_ref[0, n] = v[rows, :].T.astype(BF16)


def _mix_in(h, g, w_in_bf, qgain, kgain, tm):
    B, S, D = h.shape
    ones_bd = _group_ones(256, DIFF_QK)
    const = lambda shape: pl.BlockSpec(shape, lambda b, i: (0,) * len(shape))
    tile = lambda c: pl.BlockSpec((1, tm, c), lambda b, i: (b, i, 0))
    tile_t = pl.BlockSpec((1, tm // ATTN_T, DIFF_WIDTH, ATTN_T), lambda b, i: (b, i, 0, 0))
    shape_t = jax.ShapeDtypeStruct((B, S // ATTN_T, DIFF_WIDTH, ATTN_T), BF16)
    return pl.pallas_call(
        functools.partial(_mix_in_kernel, tm=tm),
        grid=(B, S // tm),
        in_specs=[tile(D), const((1, D)), const((D, P_IN)), const((256, 256)),
                  const((1, DIFF_WIDTH)), const((1, DIFF_WIDTH))],
        out_specs=[tile(POOL_WIDTH), tile(RWKV_COLS), tile_t, tile(DIFF_WIDTH), tile_t],
        out_shape=[jax.ShapeDtypeStruct((B, S, POOL_WIDTH), F32),
                   jax.ShapeDtypeStruct((B, S, RWKV_COLS), F32),
                   shape_t,
                   jax.ShapeDtypeStruct((B, S, DIFF_WIDTH), BF16),
                   shape_t],
        compiler_params=_cparams(("parallel", "parallel")),
        name="mix_in",
    )(h, g, w_in_bf, ones_bd, qgain, kgain)


def _pool_kernel(z_ref, halo_ref, w_ref, scale_ref, y_ref, buf_ref, *, tm):
    i = pl.program_id(1)
    u = z_ref[0]
    pad = V7X_SUBLANES
    n = tm + POOL_HALO
    halo = jnp.where(i == 0, 0.0, halo_ref[0])
    buf_ref[0:pad, :] = jnp.zeros((pad, POOL_WIDTH), F32)
    buf_ref[pad:pad + POOL_HALO, :] = halo
    buf_ref[pad + POOL_HALO:pad + n, :] = u
    lane = lax.broadcasted_iota(jnp.int32, (tm, POOL_WIDTH), 1)
    grp = lane // POOL_GDIM
    win = jnp.zeros((tm, POOL_WIDTH), F32)
    shift = 1
    for gi, w in enumerate(POOL_WINDOWS):
        while shift < w:
            cur = buf_ref[pad:pad + n, :] + buf_ref[pad - shift:pad - shift + n, :]
            buf_ref[pad:pad + n, :] = cur
            shift *= 2
        win = jnp.where(grp == gi, buf_ref[pad + POOL_HALO:pad + n, :], win)
    t = i * tm + lax.broadcasted_iota(jnp.int32, (tm, POOL_WIDTH), 0)
    wlane = jnp.left_shift(2, grp)
    count = jnp.minimum(t + 1, wlane).astype(F32)
    d = win / count - u
    y = _dot(d.astype(BF16), w_ref[...]) * scale_ref[...]
    y_ref[0] = y.astype(BF16)


def _pool(z_pool, w_bd_bf, scale, tm):
    B, S, _ = z_pool.shape
    r = tm // POOL_HALO
    return pl.pallas_call(
        functools.partial(_pool_kernel, tm=tm),
        grid=(B, S // tm),
        in_specs=[pl.BlockSpec((1, tm, POOL_WIDTH), lambda b, i: (b, i, 0)),
                  pl.BlockSpec((1, POOL_HALO, POOL_WIDTH),
                               lambda b, i: (b, jnp.maximum(i * r - 1, 0), 0)),
                  pl.BlockSpec((POOL_WIDTH, POOL_WIDTH), lambda b, i: (0, 0)),
                  pl.BlockSpec((1, POOL_WIDTH), lambda b, i: (0, 0))],
        out_specs=pl.BlockSpec((1, tm, POOL_WIDTH), lambda b, i: (b, i, 0)),
        out_shape=jax.ShapeDtypeStruct((B, S, POOL_WIDTH), BF16),
        scratch_shapes=[pltpu.VMEM((V7X_SUBLANES + POOL_HALO + tm, POOL_WIDTH), F32)],
        compiler_params=_cparams(("parallel", "parallel")),
        name="pool",
    )(z_pool, z_pool, w_bd_bf, scale)


RWKV_PAIR = 2 * RWKV_HEAD


def _bd2(x, lo_mask):
    x = x.astype(BF16)
    zero = jnp.zeros_like(x)
    return jnp.concatenate([jnp.where(lo_mask, x, zero), jnp.where(lo_mask, zero, x)], axis=0)


def _diag_blocks(full, lo_mask):
    n = full.shape[1] // RWKV_PAIR
    lo = jnp.concatenate([lo_mask] * n, axis=1) if n > 1 else lo_mask
    return jnp.where(lo, full[0:RWKV_HEAD], full[RWKV_HEAD:2 * RWKV_HEAD])


def _rwkv_kernel(z_ref, halo_ref, mu_ref, w0_ref, w2_ref, a0_ref, a2_ref, g2_ref,
                 kk_ref, ka_ref, rk_ref, lnw_ref, lnb_ref, ones_ref, tri_ref,
                 y_ref,
                 zbuf, r_s, k_s, v_s, lw_s, a_s, b_s, y_s, state, rh_s, g_s, ml_s, *, tm):
    i = pl.program_id(1)
    L = RWKV_CHUNK
    W = RWKV_WIDTH
    pad = V7X_SUBLANES

    @pl.when(i == 0)
    def _():
        state[...] = jnp.zeros_like(state)

    z = z_ref[0]
    zbuf[0:pad, :] = jnp.where(i == 0, 0.0, halo_ref[0])
    zbuf[pad:pad + tm, :] = z
    zprev = zbuf[pad - 1:pad - 1 + tm, :]
    zm = z + mu_ref[...] * (zprev - z)
    r = zm[:, 0:W]
    k = zm[:, W:2 * W]
    v = zm[:, 2 * W:3 * W]
    z6 = zm[:, 3 * W:3 * W + 128]
    gd = zm[:, 3 * W + 128:3 * W + 256]
    ones_bd = ones_ref[...]
    wl = w0_ref[...] + _dot(jnp.tanh(z6).astype(BF16), w2_ref[...])
    w = -jax.nn.softplus(-wl) - 0.5
    a = jax.nn.sigmoid(a0_ref[...] + _dot(z6.astype(BF16), a2_ref[...]))
    g = _dot(jax.nn.sigmoid(gd).astype(BF16), g2_ref[...])
    kk = k * kk_ref[...]
    kk = kk / jnp.maximum(jnp.sqrt(_gsum1(kk * kk, ones_bd)), 1e-12)
    kp = k * (1.0 + (a - 1.0) * ka_ref[...])
    r_s[...] = r
    k_s[...] = kp
    v_s[...] = v
    lw_s[...] = -jnp.exp(w)
    a_s[...] = -kk
    b_s[...] = kk * a

    PW = RWKV_PAIR
    n_pairs = W // PW
    lane2 = lax.broadcasted_iota(jnp.int32, (L, PW), 1)
    trow = lax.broadcasted_iota(jnp.int32, (L, PW), 0)
    lo_mask = lane2 < RWKV_HEAD
    jcol = jnp.bitwise_and(lane2, RWKV_HEAD - 1)
    strict = trow > jcol
    incl = trow >= jcol
    eye = trow == jcol
    bf = lambda x: x.astype(BF16)
    rows2 = lambda x, y: jnp.concatenate([x, y], axis=0)

    n_chunks = tm // L
    chains = [(ci, pg) for ci in range(n_chunks) for pg in range(n_pairs)]
    pre = []
    for ci in range(n_chunks):
        sl = slice(ci * L, (ci + 1) * L)
        lw = lw_s[sl, :]
        c_in = jnp.dot(tri_ref[...], lw, precision=lax.Precision.HIGHEST,
                       preferred_element_type=F32)
        c_tot = c_in[L - 1:L, :]
        e_neg = jnp.exp(-c_in)
        e_rem = jnp.exp(c_tot - c_in)
        pre.append(dict(
            at=a_s[sl, :] * jnp.exp(c_in - lw), rt=r_s[sl, :] * jnp.exp(c_in),
            bt=b_s[sl, :] * e_neg, kt=k_s[sl, :] * e_neg,
            bbar=b_s[sl, :] * e_rem, kbar=k_s[sl, :] * e_rem,
            v=v_s[sl, :], e_tot=jnp.exp(c_tot)))
    pair = lambda ci, pg, name: pre[ci][name][:, pg * PW:(pg + 1) * PW]

    st = []
    for ci, pg in chains:
        at, rt = pair(ci, pg, "at"), pair(ci, pg, "rt")
        lhs = bf(rows2(at, rt))
        tb = _dot_nt(lhs, _bd2(pair(ci, pg, "bt"), lo_mask))
        tk = _dot_nt(lhs, _bd2(pair(ci, pg, "kt"), lo_mask))
        tab = jnp.where(strict, tb[0:L], 0.0)
        st.append(dict(
            at=at, rt=rt, v=pair(ci, pg, "v"), tab=tab,
            trb=bf(jnp.where(incl, tb[L:2 * L], 0.0)),
            tak=jnp.where(strict, tk[0:L], 0.0), trk=jnp.where(incl, tk[L:2 * L], 0.0),
            winv=jnp.where(eye, 1.0, 0.0) + tab))
    for c in st:
        c["p"] = _dot(bf(c["tab"]), _bd2(c["tab"], lo_mask))
    for _ in range(int(math.log2(L)) - 2):
        for c in st:
            res = _dot(bf(rows2(c["p"], c["winv"])), _bd2(c["p"], lo_mask))
            c["p"] = res[0:L]
            c["winv"] = c["winv"] + res[L:2 * L]
    for c in st:
        c["winv"] = c["winv"] + _dot(bf(c["winv"]), _bd2(c["p"], lo_mask))
        c["tv"] = _dot(bf(rows2(c["tak"], c["trk"])), _bd2(c["v"], lo_mask))
    for c in st:
        x_bd = jnp.concatenate([_bd2(c["at"], lo_mask), _bd2(c["tv"][0:L], lo_mask)], axis=1)
        c["wx"] = _dot(bf(c["winv"]), x_bd)
    for c in st:
        wx = c["wx"]
        ax_bd = jnp.concatenate([_bd2(wx[:, 0:PW], lo_mask), _bd2(wx[:, PW:2 * PW], lo_mask)],
                                axis=1)
        zz = _dot(c["trb"], ax_bd)
        c["rh"] = c["rt"] + zz[:, 0:PW]
        c["yl"] = zz[:, PW:2 * PW] + c["tv"][L:2 * L]
    for (ci, pg), c in zip(chains, st):
        v = c["v"]
        lhs_t = bf(rows2(pair(ci, pg, "bbar"), pair(ci, pg, "kbar")))
        rhs_t = bf(rows2(c["wx"], jnp.concatenate([jnp.zeros_like(v), v], axis=1)))
        full = _diag_blocks(_dot_tn(lhs_t, rhs_t), lo_mask)
        gs = slice(pg * PW, (pg + 1) * PW)
        sl = slice(ci * L, (ci + 1) * L)
        rh_s[sl, gs] = bf(c["rh"])
        y_s[sl, gs] = c["yl"]
        g_s[ci, :, gs] = bf(full[:, 0:PW] + jnp.where(eye, pre[ci]["e_tot"][:, gs], 0.0))
        ml_s[ci, :, gs] = full[:, PW:2 * PW]

    m_cur = [state[:, g * PW:(g + 1) * PW] for g in range(n_pairs)]
    for ci in range(n_chunks):
        sl = slice(ci * L, (ci + 1) * L)
        for pg in range(n_pairs):
            gs = slice(pg * PW, (pg + 1) * PW)
            res = _dot(rows2(rh_s[sl, gs], g_s[ci, :, gs]), _bd2(m_cur[pg], lo_mask))
            y_s[sl, gs] = y_s[sl, gs] + res[0:L]
            m_cur[pg] = res[L:2 * L] + ml_s[ci, :, gs]
    for pg in range(n_pairs):
        state[:, pg * PW:(pg + 1) * PW] = m_cur[pg]

    y = y_s[...]
    r = r_s[...]
    kp = k_s[...]
    v = v_s[...]
    inv_n = 1.0 / RWKV_HEAD
    mean = _gsum2(y, ones_bd) * inv_n
    yc = y - mean
    var = _gsum1(yc * yc, ones_bd) * inv_n
    yn = yc * lax.rsqrt(var + RWKV_GN_EPS) * lnw_ref[...] + lnb_ref[...]
    bonus = _gsum1(r * kp * rk_ref[...], ones_bd) * v
    y_ref[0] = ((yn + bonus) * g).astype(BF16)


def _rwkv(z_rwkv, mu, w0, w2p, a0, a2p, g2, k_k, k_a, r_k, ln_w, ln_b, tm):
    B, S, C = z_rwkv.shape
    W = RWKV_WIDTH
    L = RWKV_CHUNK
    ones_bd = _group_ones(W, RWKV_HEAD)
    tri = jnp.asarray((np.arange(L)[:, None] >= np.arange(L)[None, :]).astype(np.float32))
    r = tm // V7X_SUBLANES
    const = lambda shape: pl.BlockSpec(shape, lambda b, i: (0,) * len(shape))
    vec = const((1, W))
    sq = const((W, W))
    return pl.pallas_call(
        functools.partial(_rwkv_kernel, tm=tm),
        grid=(B, S // tm),
        in_specs=[pl.BlockSpec((1, tm, C), lambda b, i: (b, i, 0)),
                  pl.BlockSpec((1, V7X_SUBLANES, C),
                               lambda b, i: (b, jnp.maximum(i * r - 1, 0), 0)),
                  const((1, C)), vec, const((128, W)), vec, const((128, W)), const((128, W)),
                  vec, vec, vec, vec, vec, sq, const((L, L))],
        out_specs=pl.BlockSpec((1, tm, W), lambda b, i: (b, i, 0)),
        out_shape=jax.ShapeDtypeStruct((B, S, W), BF16),
        scratch_shapes=[pltpu.VMEM((V7X_SUBLANES + tm, C), F32)]
        + [pltpu.VMEM((tm, W), F32) for _ in range(7)]
        + [pltpu.VMEM((RWKV_HEAD, W), F32),
           pltpu.VMEM((tm, W), BF16),
           pltpu.VMEM((tm // L, RWKV_HEAD, W), BF16),
           pltpu.VMEM((tm // L, RWKV_HEAD, W), F32)],
        compiler_params=_cparams(("parallel", "arbitrary")),
        name="rwkv",
    )(z_rwkv, z_rwkv, mu, w0, w2p, a0, a2p, g2, k_k, k_a, r_k, ln_w, ln_b, ones_bd, tri)


DIFF_ONES_ROWS = 16
DIFF_HPS = 2


def _diff_kernel(qt_ref, k_ref, vt_ref, lq1_ref, lk1_ref, lq2_ref, lk2_ref, subln_ref,
                 o_ref, bias2, m_s, acc_s, s_a, s_b, mx_a, mx_b, *, lambda_init):
    t = ATTN_T
    hw = 2 * DIFF_QK
    hp = pl.program_id(1)
    i = pl.program_id(2)
    slopes = []
    for hh in range(DIFF_HPS):
        s2 = jnp.float32(0.0)
        for hd in range(DIFF_HEADS):
            s2 = jnp.where(hp * DIFF_HPS + hh == hd,
                           2.0 ** (-8.0 * (hd + 1) / DIFF_HEADS) * LOG2E, s2)
        slopes.append(s2)

    @pl.when(i == 0)
    def _():
        kc = lax.broadcasted_iota(jnp.int32, (t, t), 0)
        qr = lax.broadcasted_iota(jnp.int32, (t, t), 1)
        vis = (kc // CHUNK) <= (qr // CHUNK)
        rel = (qr - jnp.abs(qr - kc)).astype(F32)
        for hh in range(DIFF_HPS):
            bias2[hh, 0] = slopes[hh] * kc.astype(F32)
            bias2[hh, 1] = jnp.where(vis, slopes[hh] * rel, NEG_BIG)

    row = lax.broadcasted_iota(jnp.int32, (hw, t), 0)
    qc = []
    for hh in range(DIFF_HPS):
        qt = qt_ref[0, 0, hh * hw:(hh + 1) * hw, :]
        qc.append(jnp.where(row < DIFF_QK, qt, jnp.zeros_like(qt)))
        qc.append(jnp.where(row >= DIFF_QK, qt, jnp.zeros_like(qt)))
    ones = jnp.ones((DIFF_ONES_ROWS, t), BF16)

    m_s[...] = jnp.full(m_s.shape, NEG_BIG, F32)
    acc_s[...] = jnp.zeros_like(acc_s)

    def scores(j, buf, mx):
        sl = pl.ds(pl.multiple_of(j * t, t), t)
        diag = (j == i).astype(jnp.int32)
        for hh in range(DIFF_HPS):
            kt = k_ref[0, sl, hh * hw:(hh + 1) * hw]
            bias = bias2[hh, diag]
            for c in range(2):
                sb = _dot(kt, qc[2 * hh + c]) + bias
                buf[2 * hh + c] = sb
                mx[2 * hh + c] = jnp.max(sb, axis=0, keepdims=True)

    def consume(j, buf, mx):
        for hh in range(DIFF_HPS):
            cj = slopes[hh] * ((j - i) * t).astype(F32)
            vaug = jnp.concatenate([vt_ref[0, j, hh * DIFF_VDIM:(hh + 1) * DIFF_VDIM, :], ones],
                                   axis=0)
            for c in range(2):
                n = 2 * hh + c
                m_old = m_s[n]
                m_new = jnp.maximum(m_old, mx[n] + cj)
                alpha = jnp.exp2(m_old - m_new)
                p = jnp.exp2(buf[n] - (m_new - cj))
                acc_s[n] = alpha * acc_s[n] + _dot(vaug, p.astype(BF16))
                m_s[n] = m_new

    scores(0, s_a, mx_a)

    def pair(pp, carry):
        j = 2 * pp
        scores(j + 1, s_b, mx_b)
        consume(j, s_a, mx_a)
        scores(j + 2, s_a, mx_a)
        consume(j + 1, s_b, mx_b)
        return carry

    lax.fori_loop(0, i // 2, pair, 0)

    @pl.when(i % 2 == 0)
    def _():
        consume(i, s_a, mx_a)

    @pl.when(i % 2 == 1)
    def _():
        scores(i, s_b, mx_b)
        consume(i - 1, s_a, mx_a)
        consume(i, s_b, mx_b)

    lam = (jnp.exp(jnp.sum(lq1_ref[...] * lk1_ref[...], axis=-1, keepdims=True))
           - jnp.exp(jnp.sum(lq2_ref[...] * lk2_ref[...], axis=-1, keepdims=True))
           + lambda_init)
    dv = DIFF_VDIM
    for hh in range(DIFF_HPS):
        a0, a1 = 2 * hh, 2 * hh + 1
        ot = (acc_s[a0, 0:dv, :] / acc_s[a0, dv:dv + 1, :]
              - lam * (acc_s[a1, 0:dv, :] / acc_s[a1, dv:dv + 1, :]))
        o = ot.T
        ms = jnp.mean(o * o, axis=-1, keepdims=True)
        o = o * lax.rsqrt(ms + EPS) * subln_ref[...] * (1.0 - lambda_init)
        o_ref[0, :, hh * dv:(hh + 1) * dv] = o.astype(BF16)


def _diffattn(qt, k, vt, lq1, lk1, lq2, lk2, subln, lambda_init):
    B, S, _ = k.shape
    t = ATTN_T
    n = DIFF_HPS
    hw = 2 * DIFF_QK
    nch = 2 * n
    vec = pl.BlockSpec((1, DIFF_QK), lambda b, h, i: (0, 0))
    return pl.pallas_call(
        functools.partial(_diff_kernel, lambda_init=lambda_init),
        grid=(B, DIFF_HEADS // n, S // t),
        in_specs=[pl.BlockSpec((1, 1, n * hw, t), lambda b, h, i: (b, i, h, 0)),
                  pl.BlockSpec((1, S, n * hw), lambda b, h, i: (b, 0, h)),
                  pl.BlockSpec((1, S // t, n * DIFF_VDIM, t), lambda b, h, i: (b, 0, h, 0)),
                  vec, vec, vec, vec,
                  pl.BlockSpec((1, DIFF_VDIM), lambda b, h, i: (0, 0))],
        out_specs=pl.BlockSpec((1, t, n * DIFF_VDIM), lambda b, h, i: (b, i, h)),
        out_shape=jax.ShapeDtypeStruct((B, S, DIFF_WIDTH), BF16),
        scratch_shapes=[pltpu.VMEM((n, 2, t, t), F32),
                        pltpu.VMEM((nch, 1, t), F32),
                        pltpu.VMEM((nch, DIFF_VDIM + DIFF_ONES_ROWS, t), F32),
                        pltpu.VMEM((nch, t, t), F32),
                        pltpu.VMEM((nch, t, t), F32),
                        pltpu.VMEM((nch, 1, t), F32),
                        pltpu.VMEM((nch, 1, t), F32)],
        compiler_params=_cparams(("parallel", "arbitrary", "arbitrary")),
        name="diffattn",
    )(qt, k, vt, lq1, lk1, lq2, lk2, subln)


def _memkv_kernel(mem_ref, g_ref, wk_ref, wv_ref, kn_ref, k_ref, v_ref):
    x = mem_ref[0]
    ms = jnp.mean(x * x, axis=-1, keepdims=True)
    xn = (x * lax.rsqrt(ms + EPS) * g_ref[...]).astype(BF16)
    k = _dot(xn, wk_ref[...])
    v = _dot(xn, wv_ref[...])
    ks = []
    for h in range(XA_HEADS):
        kh = k[:, h * XA_HEAD:(h + 1) * XA_HEAD]
        kms = jnp.mean(kh * kh, axis=-1, keepdims=True)
        ks.append(kh * lax.rsqrt(kms + EPS) * kn_ref[...])
    k_ref[0] = jnp.concatenate(ks, axis=1).astype(BF16)
    v_ref[0] = v.astype(BF16)


def _memkv(mem, g, wk_bf, wv_bf, k_norm):
    B, M, D = mem.shape
    const = lambda shape: pl.BlockSpec(shape, lambda b: (0,) * len(shape))
    tile = pl.BlockSpec((1, M, D), lambda b: (b, 0, 0))
    return pl.pallas_call(
        _memkv_kernel,
        grid=(B,),
        in_specs=[tile, const((1, D)), const((D, D)), const((D, D)), const((1, XA_HEAD))],
        out_specs=[tile, tile],
        out_shape=[jax.ShapeDtypeStruct((B, M, D), BF16), jax.ShapeDtypeStruct((B, M, D), BF16)],
        compiler_params=_cparams(("parallel",)),
        name="memkv",
    )(mem, g, wk_bf, wv_bf, k_norm)


def _mid_kernel(h_ref, yp_ref, yr_ref, yd_ref, wout_ref, g_ref, wq_ref, km_ref, vm_ref,
                wo_ref, qn_ref, o_ref):
    a = POOL_WIDTH
    b = POOL_WIDTH + RWKV_WIDTH
    h1 = (h_ref[0] + _dot(yp_ref[0], wout_ref[0:a, :]) + _dot(yr_ref[0], wout_ref[a:b, :])
          + _dot(yd_ref[0], wout_ref[b:D_MODEL, :]))
    ms = jnp.mean(h1 * h1, axis=-1, keepdims=True)
    xn = (h1 * lax.rsqrt(ms + EPS) * g_ref[...]).astype(BF16)
    q = _dot(xn, wq_ref[...])
    qscale = (XA_HEAD ** -0.5) * LOG2E
    outs = []
    for hd in range(XA_HEADS):
        sl = slice(hd * XA_HEAD, (hd + 1) * XA_HEAD)
        qh = q[:, sl]
        qms = jnp.mean(qh * qh, axis=-1, keepdims=True)
        qh = (qh * lax.rsqrt(qms + EPS) * (qn_ref[...] * qscale)).astype(BF16)
        s = _dot_nt(qh, km_ref[0, :, sl])
        m = jnp.max(s, axis=-1, keepdims=True)
        p = jnp.exp2(s - m)
        l = jnp.sum(p, axis=-1, keepdims=True)
        outs.append(_dot(p.astype(BF16), vm_ref[0, :, sl]) / l)
    o = jnp.concatenate(outs, axis=1).astype(BF16)
    o_ref[0] = h1 + _dot(o, wo_ref[...])


def _mid(h, yp, yr, yd, wout_bf, g, wq_bf, kmem, vmem, wo_bf, q_norm, tm):
    B, S, D = h.shape
    M = kmem.shape[1]
    const = lambda shape: pl.BlockSpec(shape, lambda b, i: (0,) * len(shape))
    tile = lambda c: pl.BlockSpec((1, tm, c), lambda b, i: (b, i, 0))
    memspec = pl.BlockSpec((1, M, D), lambda b, i: (b, 0, 0))
    return pl.pallas_call(
        _mid_kernel,
        grid=(B, S // tm),
        in_specs=[tile(D), tile(POOL_WIDTH), tile(RWKV_WIDTH), tile(DIFF_WIDTH), const((D, D)),
                  const((1, D)), const((D, D)), memspec, memspec, const((D, D)),
                  const((1, XA_HEAD))],
        out_specs=tile(D),
        out_shape=jax.ShapeDtypeStruct((B, S, D), F32),
        compiler_params=_cparams(("parallel", "parallel")),
        name="mid",
    )(h, yp, yr, yd, wout_bf, g, wq_bf, kmem, vmem, wo_bf, q_norm)


def _ffn_kernel(h_ref, g_ref, wa_ref, wb_ref, cw_ref, cb_ref, wd_ref, o_ref,
                xn_s, abuf, carry, *, tm):
    i = pl.program_id(1)
    j = pl.program_id(2)
    pad = V7X_SUBLANES

    @pl.when(j == 0)
    def _():
        x = h_ref[0]
        ms = jnp.mean(x * x, axis=-1, keepdims=True)
        xn_s[...] = (x * lax.rsqrt(ms + EPS) * g_ref[...]).astype(BF16)

    xn = xn_s[...]
    a = _dot(xn, wa_ref[...])
    b = _dot(xn, wb_ref[...])
    abuf[0:pad, :] = jnp.where(i == 0, 0.0, carry[j])
    abuf[pad:pad + tm, :] = a
    carry[j] = a[tm - pad:tm, :]
    cw = cw_ref[...]
    c = (cw[2:3, :] * a + cw[1:2, :] * abuf[pad - 1:pad - 1 + tm, :]
         + cw[0:1, :] * abuf[pad - 2:pad - 2 + tm, :] + cb_ref[...])
    gelu = 0.5 * c * (1.0 + lax.erf(c * (2.0 ** -0.5)))
    hmid = (gelu * b).astype(BF16)
    contrib = _dot(hmid, wd_ref[...])

    @pl.when(j == 0)
    def _():
        o_ref[0] = h_ref[0] + contrib

    @pl.when(j != 0)
    def _():
        o_ref[0] = o_ref[0] + contrib


def _ffn(h, g, wup_bf, conv_w, conv_b, wdown_bf, tm, nf):
    B, S, D = h.shape
    tf = D_FF // nf
    return pl.pallas_call(
        functools.partial(_ffn_kernel, tm=tm),
        grid=(B, S // tm, nf),
        in_specs=[pl.BlockSpec((1, tm, D), lambda b, i, j: (b, i, 0)),
                  pl.BlockSpec((1, D), lambda b, i, j: (0, 0)),
                  pl.BlockSpec((D, tf), lambda b, i, j: (0, j)),
                  pl.BlockSpec((D, tf), lambda b, i, j: (0, j + nf)),
                  pl.BlockSpec((CONV_W, tf), lambda b, i, j: (0, j)),
                  pl.BlockSpec((1, tf), lambda b, i, j: (0, j)),
                  pl.BlockSpec((tf, D), lambda b, i, j: (j, 0))],
        out_specs=pl.BlockSpec((1, tm, D), lambda b, i, j: (b, i, 0)),
        out_shape=jax.ShapeDtypeStruct((B, S, D), F32),
        scratch_shapes=[pltpu.VMEM((tm, D), BF16),
                        pltpu.VMEM((V7X_SUBLANES + tm, tf), F32),
                        pltpu.VMEM((nf, V7X_SUBLANES, tf), F32)],
        compiler_params=_cparams(("parallel", "arbitrary", "arbitrary")),
        name="ffn",
    )(h, g, wup_bf, wup_bf, conv_w, conv_b, wdown_bf)


def _tiles(S):
    pick = lambda pref: max(c for c in (64, 128, 256, 512, 1024) if c <= pref and S % c == 0)
    assert S % ATTN_T == 0
    return dict(mix=pick(512), pool=pick(1024), rwkv=pick(512), mid=pick(512),
                ffn=pick(512))


def _block_diag(blocks):
    n = len(blocks)
    rows = []
    for i, blk in enumerate(blocks):
        rows.append(jnp.concatenate(
            [blk if j == i else jnp.zeros_like(blk) for j in range(n)], axis=1))
    return jnp.concatenate(rows, axis=0)


def kernel(x, mem, mix_norm_g, w_in, pool_w, pool_scale, rwkv_mu, rwkv_w0, rwkv_w2, rwkv_a0, rwkv_a2, rwkv_g2, rwkv_k_k, rwkv_k_a, rwkv_r_k, rwkv_ln_w, rwkv_ln_b, diff_q_norm, diff_k_norm, diff_lq1, diff_lk1, diff_lq2, diff_lk2, diff_subln, w_out, xa_norm_g, mem_norm_g, xa_wq, xa_wk, xa_wv, xa_wo, xa_q_norm, xa_k_norm, ffn_norm_g, ffn_w_up, ffn_conv_w, ffn_conv_b, ffn_w_down):
    B, S, D = x.shape
    depth = w_in.shape[0]
    tl = _tiles(S)
    row = lambda a: a.reshape(1, -1).astype(F32)
    h = x
    for l in range(depth):
        lambda_init = 0.8 - 0.6 * math.exp(-0.3 * l)
        qgain = row(jnp.tile(diff_q_norm[l].reshape(-1), DIFF_HEADS)) * (DIFF_QK ** -0.5 * LOG2E)
        kgain = row(jnp.tile(diff_k_norm[l].reshape(-1), DIFF_HEADS))
        zeros64 = jnp.zeros((64, RWKV_WIDTH), F32)
        w2p = jnp.concatenate([rwkv_w2[l], zeros64], axis=0).astype(BF16)
        a2p = jnp.concatenate([zeros64, rwkv_a2[l]], axis=0).astype(BF16)
        pool_bd = _block_diag([pool_w[l, gi] for gi in range(len(POOL_WINDOWS))]).astype(BF16)

        z_pool, z_rwkv, qd, kd, vd = _mix_in(h, row(mix_norm_g[l]), w_in[l].astype(BF16),
                                             qgain, kgain, tl["mix"])
        y_pool = _pool(z_pool, pool_bd, row(pool_scale[l]), tl["pool"])
        y_rwkv = _rwkv(z_rwkv, row(rwkv_mu[l]), row(rwkv_w0[l]), w2p, row(rwkv_a0[l]), a2p,
                       rwkv_g2[l].astype(BF16), row(rwkv_k_k[l]), row(rwkv_k_a[l]),
                       row(rwkv_r_k[l]), row(rwkv_ln_w[l]), row(rwkv_ln_b[l]), tl["rwkv"])
        y_diff = _diffattn(qd, kd, vd, row(diff_lq1[l]), row(diff_lk1[l]), row(diff_lq2[l]),
                           row(diff_lk2[l]), row(diff_subln[l]), lambda_init)
        kmem, vmem = _memkv(mem, row(mem_norm_g[l]), xa_wk[l].astype(BF16),
                            xa_wv[l].astype(BF16), row(xa_k_norm[l]))
        h = _mid(h, y_pool, y_rwkv, y_diff, w_out[l].astype(BF16), row(xa_norm_g[l]),
                 xa_wq[l].astype(BF16), kmem, vmem, xa_wo[l].astype(BF16), row(xa_q_norm[l]),
                 tl["mid"])
        h = _ffn(h, row(ffn_norm_g[l]), ffn_w_up[l].astype(BF16), ffn_conv_w[l].astype(F32),
                 row(ffn_conv_b[l]), ffn_w_down[l].astype(BF16), tl["ffn"], 2)
    return h
```

```python
import functools
import math

import jax
import jax.numpy as jnp
import numpy as np
from jax import lax
from jax.experimental import pallas as pl
from jax.experimental.pallas import tpu as pltpu

F32 = jnp.float32
BF16 = jnp.bfloat16

D_MODEL = 1024
EPS = 1e-6
CHUNK = 64

POOL_WIDTH = 256
POOL_GDIM = 64
POOL_WINDOWS = (2, 4, 8, 16)
POOL_HALO = 16

RWKV_WIDTH = 256
RWKV_HEAD = 64
RWKV_COLS = 1024
RWKV_GN_EPS = 64e-5
RWKV_CHUNK = 64

DIFF_WIDTH = 512
DIFF_HEADS = 4
DIFF_VDIM = 128
DIFF_QK = 64
P_IN = POOL_WIDTH + RWKV_COLS + 3 * DIFF_WIDTH

XA_HEADS = 4
XA_HEAD = 256
D_FF = 2816
CONV_W = 3
OUT_BLOCK = 256

LOG2E = math.log2(math.e)
NEG_BIG = -1e30
ATTN_T = 512

V7X_SUBLANES = 8
V7X_VMEM_LIMIT = 52 * 1024 * 1024


def _cparams(sem):
    return pltpu.CompilerParams(dimension_semantics=sem, vmem_limit_bytes=V7X_VMEM_LIMIT)


def _dot(a, b):
    return jnp.dot(a, b, preferred_element_type=F32)


def _dot_nt(a, b):
    return lax.dot_general(a, b, (((1,), (1,)), ((), ())), preferred_element_type=F32)


def _dot_tn(a, b):
    return lax.dot_general(a, b, (((0,), (0,)), ((), ())), preferred_element_type=F32)


def _group_ones(width, group):
    idx = np.arange(width) // group
    return jnp.asarray((idx[:, None] == idx[None, :]).astype(np.float32), dtype=BF16)


def _gsum1(x, ones_bd):
    w = ones_bd.shape[0]
    parts = [_dot(x[:, i:i + w].astype(BF16), ones_bd) for i in range(0, x.shape[1], w)]
    return parts[0] if len(parts) == 1 else jnp.concatenate(parts, axis=1)


def _gsum2(x, ones_bd):
    hi = x.astype(BF16)
    lo = (x - hi.astype(F32)).astype(BF16)
    return _dot(hi, ones_bd) + _dot(lo, ones_bd)


def _pool_mixer(u, halo, buf_ref, t0, w_bd, scale):
    tm = u.shape[0]
    pad = V7X_SUBLANES
    n = tm + POOL_HALO
    buf_ref[0:pad, :] = jnp.zeros((pad, POOL_WIDTH), F32)
    buf_ref[pad:pad + POOL_HALO, :] = halo
    buf_ref[pad + POOL_HALO:pad + n, :] = u
    lane = lax.broadcasted_iota(jnp.int32, (tm, POOL_WIDTH), 1)
    grp = lane // POOL_GDIM
    win = jnp.zeros((tm, POOL_WIDTH), F32)
    shift = 1
    for gi, w in enumerate(POOL_WINDOWS):
        while shift < w:
            cur = buf_ref[pad:pad + n, :] + buf_ref[pad - shift:pad - shift + n, :]
            buf_ref[pad:pad + n, :] = cur
            shift *= 2
        win = jnp.where(grp == gi, buf_ref[pad + POOL_HALO:pad + n, :], win)
    t = t0 + lax.broadcasted_iota(jnp.int32, (tm, POOL_WIDTH), 0)
    wlane = jnp.left_shift(2, grp)
    count = jnp.minimum(t + 1, wlane).astype(F32)
    d = win / count - u
    return _dot(d.astype(BF16), w_bd) * scale


def _mix_in_kernel(x_ref, g_ref, w_ref, ones_ref, qg_ref, kg_ref, pw_ref, ps_ref,
                   yp_ref, zr_ref, q_ref, k_ref, v_ref, pbuf, phalo, *, tm):
    i = pl.program_id(1)
    x = x_ref[0]
    ms = jnp.mean(x * x, axis=-1, keepdims=True)
    xn = (x * lax.rsqrt(ms + EPS) * g_ref[...]).astype(BF16)
    z = _dot(xn, w_ref[...])
    u = z[:, :POOL_WIDTH]
    halo = jnp.where(i == 0, 0.0, phalo[...])
    phalo[...] = u[tm - POOL_HALO:tm, :]
    yp_ref[0] = _pool_mixer(u, halo, pbuf, i * tm, pw_ref[...], ps_ref[...]).astype(BF16)
    zr_ref[0] = z[:, POOL_WIDTH:POOL_WIDTH + RWKV_COLS]
    o = POOL_WIDTH + RWKV_COLS
    q = z[:, o:o + DIFF_WIDTH]
    k = z[:, o + DIFF_WIDTH:o + 2 * DIFF_WIDTH]
    v = z[:, o + 2 * DIFF_WIDTH:o + 3 * DIFF_WIDTH]
    ones_bd = ones_ref[...]
    qss = _gsum1(q * q, ones_bd) * (1.0 / DIFF_QK)
    kss = _gsum1(k * k, ones_bd) * (1.0 / DIFF_QK)
    qn = q * lax.rsqrt(qss + EPS) * qg_ref[...]
    k_ref[0] = (k * lax.rsqrt(kss + EPS) * kg_ref[...]).astype(BF16)
    for n in range(tm // ATTN_T):
        rows = slice(n * ATTN_T, (n + 1) * ATTN_T)
        q_ref[0, n] = qn[rows, :].T.astype(BF16)
        v_ref[0, n] = v[rows, :].T.astype(BF16)


def _mix_in(h, g, w_in_bf, qgain, kgain, pool_bd_bf, pool_scale, tm):
    B, S, D = h.shape
    ones_bd = _group_ones(256, DIFF_QK)
    const = lambda shape: pl.BlockSpec(shape, lambda b, i: (0,) * len(shape))
    tile = lambda c: pl.BlockSpec((1, tm, c), lambda b, i: (b, i, 0))
    tile_t = pl.BlockSpec((1, tm // ATTN_T, DIFF_WIDTH, ATTN_T), lambda b, i: (b, i, 0, 0))
    shape_t = jax.ShapeDtypeStruct((B, S // ATTN_T, DIFF_WIDTH, ATTN_T), BF16)
    return pl.pallas_call(
        functools.partial(_mix_in_kernel, tm=tm),
        grid=(B, S // tm),
        in_specs=[tile(D), const((1, D)), const((D, P_IN)), const((256, 256)),
                  const((1, DIFF_WIDTH)), const((1, DIFF_WIDTH)),
                  const((POOL_WIDTH, POOL_WIDTH)), const((1, POOL_WIDTH))],
        out_specs=[tile(POOL_WIDTH), tile(RWKV_COLS), tile_t, tile(DIFF_WIDTH), tile_t],
        out_shape=[jax.ShapeDtypeStruct((B, S, POOL_WIDTH), BF16),
                   jax.ShapeDtypeStruct((B, S, RWKV_COLS), F32),
                   shape_t,
                   jax.ShapeDtypeStruct((B, S, DIFF_WIDTH), BF16),
                   shape_t],
        scratch_shapes=[pltpu.VMEM((V7X_SUBLANES + POOL_HALO + tm, POOL_WIDTH), F32),
                        pltpu.VMEM((POOL_HALO, POOL_WIDTH), F32)],
        compiler_params=_cparams(("parallel", "arbitrary")),
        name="mix_in",
    )(h, g, w_in_bf, ones_bd, qgain, kgain, pool_bd_bf, pool_scale)


RWKV_PAIR = 2 * RWKV_HEAD


def _bd2(x, lo_mask):
    x = x.astype(BF16)
    zero = jnp.zeros_like(x)
    return jnp.concatenate([jnp.where(lo_mask, x, zero), jnp.where(lo_mask, zero, x)], axis=0)


def _diag_blocks(full, lo_mask):
    n = full.shape[1] // RWKV_PAIR
    lo = jnp.concatenate([lo_mask] * n, axis=1) if n > 1 else lo_mask
    return jnp.where(lo, full[0:RWKV_HEAD], full[RWKV_HEAD:2 * RWKV_HEAD])


def _rwkv_kernel(z_ref, halo_ref, mu_ref, w0_ref, w2_ref, a0_ref, a2_ref, g2_ref,
                 kk_ref, ka_ref, rk_ref, lnw_ref, lnb_ref, ones_ref, tri_ref,
                 y_ref,
                 zbuf, r_s, k_s, v_s, lw_s, a_s, b_s, y_s, state, rh_s, g_s, ml_s, *, tm):
    i = pl.program_id(1)
    L = RWKV_CHUNK
    W = RWKV_WIDTH
    pad = V7X_SUBLANES

    @pl.when(i == 0)
    def _():
        state[...] = jnp.zeros_like(state)

    z = z_ref[0]
    zbuf[0:pad, :] = jnp.where(i == 0, 0.0, halo_ref[0])
    zbuf[pad:pad + tm, :] = z
    zprev = zbuf[pad - 1:pad - 1 + tm, :]
    zm = z + mu_ref[...] * (zprev - z)
    r = zm[:, 0:W]
    k = zm[:, W:2 * W]
    v = zm[:, 2 * W:3 * W]
    z6 = zm[:, 3 * W:3 * W + 128]
    gd = zm[:, 3 * W + 128:3 * W + 256]
    ones_bd = ones_ref[...]
    wl = w0_ref[...] + _dot(jnp.tanh(z6).astype(BF16), w2_ref[...])
    w = -jax.nn.softplus(-wl) - 0.5
    a = jax.nn.sigmoid(a0_ref[...] + _dot(z6.astype(BF16), a2_ref[...]))
    g = _dot(jax.nn.sigmoid(gd).astype(BF16), g2_ref[...])
    kk = k * kk_ref[...]
    kk = kk * lax.rsqrt(jnp.maximum(_gsum1(kk * kk, ones_bd), 1e-24))
    kp = k * (1.0 + (a - 1.0) * ka_ref[...])
    r_s[...] = r
    k_s[...] = kp
    v_s[...] = v
    lw_all = -jnp.exp(w)
    lw_s[...] = lw_all
    lw3 = []
    rest = lw_all
    for _ in range(3):
        term = rest.astype(BF16)
        lw3.append(term)
        rest = rest - term.astype(F32)
    a_s[...] = -kk
    b_s[...] = kk * a

    PW = RWKV_PAIR
    n_pairs = W // PW
    lane2 = lax.broadcasted_iota(jnp.int32, (L, PW), 1)
    trow = lax.broadcasted_iota(jnp.int32, (L, PW), 0)
    lo_mask = lane2 < RWKV_HEAD
    jcol = jnp.bitwise_and(lane2, RWKV_HEAD - 1)
    strict = trow > jcol
    incl = trow >= jcol
    eye = trow == jcol
    bf = lambda x: x.astype(BF16)
    rows2 = lambda x, y: jnp.concatenate([x, y], axis=0)

    n_chunks = tm // L
    chains = [(ci, pg) for ci in range(n_chunks) for pg in range(n_pairs)]
    pre = []
    for ci in range(n_chunks):
        sl = slice(ci * L, (ci + 1) * L)
        lw = lw_s[sl, :]
        c_in = (_dot(tri_ref[...], lw3[0][sl, :]) + _dot(tri_ref[...], lw3[1][sl, :])
                + _dot(tri_ref[...], lw3[2][sl, :]))
        c_tot = c_in[L - 1:L, :]
        e_neg = jnp.exp(-c_in)
        e_rem = jnp.exp(c_tot - c_in)
        pre.append(dict(
            at=a_s[sl, :] * jnp.exp(c_in - lw), rt=r_s[sl, :] * jnp.exp(c_in),
            bt=b_s[sl, :] * e_neg, kt=k_s[sl, :] * e_neg,
            bbar=b_s[sl, :] * e_rem, kbar=k_s[sl, :] * e_rem,
            v=v_s[sl, :], e_tot=jnp.exp(c_tot)))
    pair = lambda ci, pg, name: pre[ci][name][:, pg * PW:(pg + 1) * PW]

    st = []
    for ci, pg in chains:
        at, rt = pair(ci, pg, "at"), pair(ci, pg, "rt")
        lhs = bf(rows2(at, rt))
        tb = _dot_nt(lhs, _bd2(pair(ci, pg, "bt"), lo_mask))
        tk = _dot_nt(lhs, _bd2(pair(ci, pg, "kt"), lo_mask))
        tab = jnp.where(strict, tb[0:L], 0.0)
        st.append(dict(
            at=at, rt=rt, v=pair(ci, pg, "v"), tab=tab,
            trb=bf(jnp.where(incl, tb[L:2 * L], 0.0)),
            tak=jnp.where(strict, tk[0:L], 0.0), trk=jnp.where(incl, tk[L:2 * L], 0.0),
            winv=jnp.where(eye, 1.0, 0.0) + tab))
    for c in st:
        c["p"] = _dot(bf(c["tab"]), _bd2(c["tab"], lo_mask))
    for _ in range(int(math.log2(L)) - 2):
        for c in st:
            res = _dot(bf(rows2(c["p"], c["winv"])), _bd2(c["p"], lo_mask))
            c["p"] = res[0:L]
            c["winv"] = c["winv"] + res[L:2 * L]
    for c in st:
        c["winv"] = c["winv"] + _dot(bf(c["winv"]), _bd2(c["p"], lo_mask))
        c["tv"] = _dot(bf(rows2(c["tak"], c["trk"])), _bd2(c["v"], lo_mask))
    for c in st:
        x_bd = jnp.concatenate([_bd2(c["at"], lo_mask), _bd2(c["tv"][0:L], lo_mask)], axis=1)
        c["wx"] = _dot(bf(c["winv"]), x_bd)
    for c in st:
        wx = c["wx"]
        ax_bd = jnp.concatenate([_bd2(wx[:, 0:PW], lo_mask), _bd2(wx[:, PW:2 * PW], lo_mask)],
                                axis=1)
        zz = _dot(c["trb"], ax_bd)
        c["rh"] = c["rt"] + zz[:, 0:PW]
        c["yl"] = zz[:, PW:2 * PW] + c["tv"][L:2 * L]
    for (ci, pg), c in zip(chains, st):
        v = c["v"]
        lhs_t = bf(rows2(pair(ci, pg, "bbar"), pair(ci, pg, "kbar")))
        rhs_t = bf(rows2(c["wx"], jnp.concatenate([jnp.zeros_like(v), v], axis=1)))
        full = _diag_blocks(_dot_tn(lhs_t, rhs_t), lo_mask)
        gs = slice(pg * PW, (pg + 1) * PW)
        sl = slice(ci * L, (ci + 1) * L)
        rh_s[sl, gs] = bf(c["rh"])
        y_s[sl, gs] = c["yl"]
        g_s[ci, :, gs] = bf(full[:, 0:PW] + jnp.where(eye, pre[ci]["e_tot"][:, gs], 0.0))
        ml_s[ci, :, gs] = full[:, PW:2 * PW]

    m_cur = [state[:, g * PW:(g + 1) * PW] for g in range(n_pairs)]
    for ci in range(n_chunks):
        sl = slice(ci * L, (ci + 1) * L)
        for pg in range(n_pairs):
            gs = slice(pg * PW, (pg + 1) * PW)
            res = _dot(rows2(rh_s[sl, gs], g_s[ci, :, gs]), _bd2(m_cur[pg], lo_mask))
            y_s[sl, gs] = y_s[sl, gs] + res[0:L]
            m_cur[pg] = res[L:2 * L] + ml_s[ci, :, gs]
    for pg in range(n_pairs):
        state[:, pg * PW:(pg + 1) * PW] = m_cur[pg]

    y = y_s[...]
    r = r_s[...]
    kp = k_s[...]
    v = v_s[...]
    inv_n = 1.0 / RWKV_HEAD
    mean = _gsum2(y, ones_bd) * inv_n
    yc = y - mean
    var = _gsum1(yc * yc, ones_bd) * inv_n
    yn = yc * lax.rsqrt(var + RWKV_GN_EPS) * lnw_ref[...] + lnb_ref[...]
    bonus = _gsum1(r * kp * rk_ref[...], ones_bd) * v
    y_ref[0] = ((yn + bonus) * g).astype(BF16)


def _rwkv(z_rwkv, mu, w0, w2p, a0, a2p, g2, k_k, k_a, r_k, ln_w, ln_b, tm):
    B, S, C = z_rwkv.shape
    W = RWKV_WIDTH
    L = RWKV_CHUNK
    ones_bd = _group_ones(W, RWKV_HEAD)
    tri = jnp.asarray((np.arange(L)[:, None] >= np.arange(L)[None, :]).astype(np.float32),
                      dtype=BF16)
    r = tm // V7X_SUBLANES
    const = lambda shape: pl.BlockSpec(shape, lambda b, i: (0,) * len(shape))
    vec = const((1, W))
    sq = const((W, W))
    return pl.pallas_call(
        functools.partial(_rwkv_kernel, tm=tm),
        grid=(B, S // tm),
        in_specs=[pl.BlockSpec((1, tm, C), lambda b, i: (b, i, 0)),
                  pl.BlockSpec((1, V7X_SUBLANES, C),
                               lambda b, i: (b, jnp.maximum(i * r - 1, 0), 0)),
                  const((1, C)), vec, const((128, W)), vec, const((128, W)), const((128, W)),
                  vec, vec, vec, vec, vec, sq, const((L, L))],
        out_specs=pl.BlockSpec((1, tm, W), lambda b, i: (b, i, 0)),
        out_shape=jax.ShapeDtypeStruct((B, S, W), BF16),
        scratch_shapes=[pltpu.VMEM((V7X_SUBLANES + tm, C), F32)]
        + [pltpu.VMEM((tm, W), F32) for _ in range(7)]
        + [pltpu.VMEM((RWKV_HEAD, W), F32),
           pltpu.VMEM((tm, W), BF16),
           pltpu.VMEM((tm // L, RWKV_HEAD, W), BF16),
           pltpu.VMEM((tm // L, RWKV_HEAD, W), F32)],
        compiler_params=_cparams(("parallel", "arbitrary")),
        name="rwkv",
    )(z_rwkv, z_rwkv, mu, w0, w2p, a0, a2p, g2, k_k, k_a, r_k, ln_w, ln_b, ones_bd, tri)


DIFF_ONES_ROWS = 16
DIFF_HPS = 2


def _diff_kernel(qt_ref, k_ref, vt_ref, lq1_ref, lk1_ref, lq2_ref, lk2_ref, subln_ref,
                 o_ref, bias2, m_s, acc_s, s_a, s_b, mx_a, mx_b, *, lambda_init):
    t = ATTN_T
    hw = 2 * DIFF_QK
    hp = pl.program_id(1)
    i = pl.program_id(2)
    slopes = []
    for hh in range(DIFF_HPS):
        s2 = jnp.float32(0.0)
        for hd in range(DIFF_HEADS):
            s2 = jnp.where(hp * DIFF_HPS + hh == hd,
                           2.0 ** (-8.0 * (hd + 1) / DIFF_HEADS) * LOG2E, s2)
        slopes.append(s2)

    @pl.when(i == 0)
    def _():
        kc = lax.broadcasted_iota(jnp.int32, (t, t), 0)
        qr = lax.broadcasted_iota(jnp.int32, (t, t), 1)
        vis = (kc // CHUNK) <= (qr // CHUNK)
        rel = (qr - jnp.abs(qr - kc)).astype(F32)
        for hh in range(DIFF_HPS):
            bias2[hh, 0] = slopes[hh] * kc.astype(F32)
            bias2[hh, 1] = jnp.where(vis, slopes[hh] * rel, NEG_BIG)

    row = lax.broadcasted_iota(jnp.int32, (hw, t), 0)
    qc = []
    for hh in range(DIFF_HPS):
        qt = qt_ref[0, 0, hh * hw:(hh + 1) * hw, :]
        qc.append(jnp.where(row < DIFF_QK, qt, jnp.zeros_like(qt)))
        qc.append(jnp.where(row >= DIFF_QK, qt, jnp.zeros_like(qt)))
    ones = jnp.ones((DIFF_ONES_ROWS, t), BF16)

    m_s[...] = jnp.full(m_s.shape, NEG_BIG, F32)
    acc_s[...] = jnp.zeros_like(acc_s)

    n_chains = 2 * DIFF_HPS

    def scores(n, j, buf, mx):
        hh = n // 2
        sl = pl.ds(pl.multiple_of(j * t, t), t)
        kt = k_ref[0, sl, hh * hw:(hh + 1) * hw]
        sb = _dot(kt, qc[n]) + bias2[hh, (j == i).astype(jnp.int32)]
        buf[n] = sb
        mx[n] = jnp.max(sb, axis=0, keepdims=True)

    def consume(n, j, buf, mx):
        hh = n // 2
        cj = slopes[hh] * ((j - i) * t).astype(F32)
        vaug = jnp.concatenate([vt_ref[0, j, hh * DIFF_VDIM:(hh + 1) * DIFF_VDIM, :], ones],
                               axis=0)
        m_old = m_s[n]
        m_new = jnp.maximum(m_old, mx[n] + cj)
        alpha = jnp.exp2(m_old - m_new)
        p = jnp.exp2(buf[n] - (m_new - cj))
        acc_s[n] = alpha * acc_s[n] + _dot(vaug, p.astype(BF16))
        m_s[n] = m_new

    def step(j_next, nxt, j_cur, cur):
        for n in range(n_chains):
            if nxt is not None:
                scores(n, j_next, *nxt)
            if cur is not None:
                consume(n, j_cur, *cur)

    buf_a, buf_b = (s_a, mx_a), (s_b, mx_b)
    step(0, buf_a, None, None)

    def pair(pp, carry):
        j = 2 * pp
        step(j + 1, buf_b, j, buf_a)
        step(j + 2, buf_a, j + 1, buf_b)
        return carry

    lax.fori_loop(0, i // 2, pair, 0)

    @pl.when(i % 2 == 0)
    def _():
        step(None, None, i, buf_a)

    @pl.when(i % 2 == 1)
    def _():
        step(i, buf_b, i - 1, buf_a)
        step(None, None, i, buf_b)

    lam = (jnp.exp(jnp.sum(lq1_ref[...] * lk1_ref[...], axis=-1, keepdims=True))
           - jnp.exp(jnp.sum(lq2_ref[...] * lk2_ref[...], axis=-1, keepdims=True))
           + lambda_init)
    dv = DIFF_VDIM
    for hh in range(DIFF_HPS):
        a0, a1 = 2 * hh, 2 * hh + 1
        ot = (acc_s[a0, 0:dv, :] / acc_s[a0, dv:dv + 1, :]
              - lam * (acc_s[a1, 0:dv, :] / acc_s[a1, dv:dv + 1, :]))
        o = ot.T
        ms = jnp.mean(o * o, axis=-1, keepdims=True)
        o = o * lax.rsqrt(ms + EPS) * subln_ref[...] * (1.0 - lambda_init)
        o_ref[0, :, hh * dv:(hh + 1) * dv] = o.astype(BF16)


def _diffattn(qt, k, vt, lq1, lk1, lq2, lk2, subln, lambda_init):
    B, S, _ = k.shape
    t = ATTN_T
    n = DIFF_HPS
    hw = 2 * DIFF_QK
    nch = 2 * n
    vec = pl.BlockSpec((1, DIFF_QK), lambda b, h, i: (0, 0))
    return pl.pallas_call(
        functools.partial(_diff_kernel, lambda_init=lambda_init),
        grid=(B, DIFF_HEADS // n, S // t),
        in_specs=[pl.BlockSpec((1, 1, n * hw, t), lambda b, h, i: (b, i, h, 0)),
                  pl.BlockSpec((1, S, n * hw), lambda b, h, i: (b, 0, h)),
                  pl.BlockSpec((1, S // t, n * DIFF_VDIM, t), lambda b, h, i: (b, 0, h, 0)),
                  vec, vec, vec, vec,
                  pl.BlockSpec((1, DIFF_VDIM), lambda b, h, i: (0, 0))],
        out_specs=pl.BlockSpec((1, t, n * DIFF_VDIM), lambda b, h, i: (b, i, h)),
        out_shape=jax.ShapeDtypeStruct((B, S, DIFF_WIDTH), BF16),
        scratch_shapes=[pltpu.VMEM((n, 2, t, t), F32),
                        pltpu.VMEM((nch, 1, t), F32),
                        pltpu.VMEM((nch, DIFF_VDIM + DIFF_ONES_ROWS, t), F32),
                        pltpu.VMEM((nch, t, t), F32),
                        pltpu.VMEM((nch, t, t), F32),
                        pltpu.VMEM((nch, 1, t), F32),
                        pltpu.VMEM((nch, 1, t), F32)],
        compiler_params=_cparams(("parallel", "arbitrary", "arbitrary")),
        name="diffattn",
    )(qt, k, vt, lq1, lk1, lq2, lk2, subln)


def _memkv_kernel(mem_ref, g_ref, wk_ref, wv_ref, kn_ref, k_ref, v_ref):
    x = mem_ref[0]
    ms = jnp.mean(x * x, axis=-1, keepdims=True)
    xn = (x * lax.rsqrt(ms + EPS) * g_ref[...]).astype(BF16)
    k = _dot(xn, wk_ref[...])
    v = _dot(xn, wv_ref[...])
    ks = []
    for h in range(XA_HEADS):
        kh = k[:, h * XA_HEAD:(h + 1) * XA_HEAD]
        kms = jnp.mean(kh * kh, axis=-1, keepdims=True)
        ks.append(kh * lax.rsqrt(kms + EPS) * kn_ref[...])
    k_ref[0] = jnp.concatenate(ks, axis=1).astype(BF16)
    v_ref[0] = v.astype(BF16)


def _memkv(mem, g, wk_bf, wv_bf, k_norm):
    B, M, D = mem.shape
    const = lambda shape: pl.BlockSpec(shape, lambda b: (0,) * len(shape))
    tile = pl.BlockSpec((1, M, D), lambda b: (b, 0, 0))
    return pl.pallas_call(
        _memkv_kernel,
        grid=(B,),
        in_specs=[tile, const((1, D)), const((D, D)), const((D, D)), const((1, XA_HEAD))],
        out_specs=[tile, tile],
        out_shape=[jax.ShapeDtypeStruct((B, M, D), BF16), jax.ShapeDtypeStruct((B, M, D), BF16)],
        compiler_params=_cparams(("parallel",)),
        name="memkv",
    )(mem, g, wk_bf, wv_bf, k_norm)


def _mid_kernel(h_ref, yp_ref, yr_ref, yd_ref, wout_ref, g_ref, wq_ref, km_ref, vm_ref,
                wo_ref, qn_ref, o_ref):
    a = POOL_WIDTH
    b = POOL_WIDTH + RWKV_WIDTH
    h1 = (h_ref[0] + _dot(yp_ref[0], wout_ref[0:a, :]) + _dot(yr_ref[0], wout_ref[a:b, :])
          + _dot(yd_ref[0], wout_ref[b:D_MODEL, :]))
    ms = jnp.mean(h1 * h1, axis=-1, keepdims=True)
    xn = (h1 * lax.rsqrt(ms + EPS) * g_ref[...]).astype(BF16)
    q = _dot(xn, wq_ref[...])
    qscale = (XA_HEAD ** -0.5) * LOG2E
    heads = [slice(hd * XA_HEAD, (hd + 1) * XA_HEAD) for hd in range(XA_HEADS)]
    scores = []
    for sl in heads:
        qh = q[:, sl]
        qms = jnp.mean(qh * qh, axis=-1, keepdims=True)
        qh = (qh * lax.rsqrt(qms + EPS) * (qn_ref[...] * qscale)).astype(BF16)
        scores.append(_dot_nt(qh, km_ref[0, :, sl]))
    probs, sums = [], []
    for s in scores:
        p = jnp.exp2(s - jnp.max(s, axis=-1, keepdims=True))
        sums.append(jnp.sum(p, axis=-1, keepdims=True))
        probs.append(p.astype(BF16))
    outs = [_dot(p, vm_ref[0, :, sl]) / l for p, l, sl in zip(probs, sums, heads)]
    o = jnp.concatenate(outs, axis=1).astype(BF16)
    for n in range(0, D_MODEL, OUT_BLOCK):
        cols = slice(n, n + OUT_BLOCK)
        o_ref[0, :, cols] = h1[:, cols] + _dot(o, wo_ref[:, cols])


def _mid(h, yp, yr, yd, wout_bf, g, wq_bf, kmem, vmem, wo_bf, q_norm, tm):
    B, S, D = h.shape
    M = kmem.shape[1]
    const = lambda shape: pl.BlockSpec(shape, lambda b, i: (0,) * len(shape))
    tile = lambda c: pl.BlockSpec((1, tm, c), lambda b, i: (b, i, 0))
    memspec = pl.BlockSpec((1, M, D), lambda b, i: (b, 0, 0))
    return pl.pallas_call(
        _mid_kernel,
        grid=(B, S // tm),
        in_specs=[tile(D), tile(POOL_WIDTH), tile(RWKV_WIDTH), tile(DIFF_WIDTH), const((D, D)),
                  const((1, D)), const((D, D)), memspec, memspec, const((D, D)),
                  const((1, XA_HEAD))],
        out_specs=tile(D),
        out_shape=jax.ShapeDtypeStruct((B, S, D), F32),
        compiler_params=_cparams(("parallel", "parallel")),
        name="mid",
    )(h, yp, yr, yd, wout_bf, g, wq_bf, kmem, vmem, wo_bf, q_norm)


def _ffn_kernel(h_ref, g_ref, wa_ref, wb_ref, cw_ref, cb_ref, wd_ref, o_ref,
                xn_s, abuf, carry, *, tm):
    i = pl.program_id(1)
    j = pl.program_id(2)
    pad = V7X_SUBLANES

    @pl.when(j == 0)
    def _():
        x = h_ref[0]
        ms = jnp.mean(x * x, axis=-1, keepdims=True)
        xn_s[...] = (x * lax.rsqrt(ms + EPS) * g_ref[...]).astype(BF16)
        o_ref[0] = x

    xn = xn_s[...]
    a = _dot(xn, wa_ref[...])
    b = _dot(xn, wb_ref[...])
    abuf[0:pad, :] = jnp.where(i == 0, 0.0, carry[j])
    abuf[pad:pad + tm, :] = a
    carry[j] = a[tm - pad:tm, :]
    cw = cw_ref[...]
    c = (cw[2:3, :] * a + cw[1:2, :] * abuf[pad - 1:pad - 1 + tm, :]
         + cw[0:1, :] * abuf[pad - 2:pad - 2 + tm, :] + cb_ref[...])
    gelu = 0.5 * c * (1.0 + lax.erf(c * (2.0 ** -0.5)))
    hmid = (gelu * b).astype(BF16)
    for n in range(0, D_MODEL, OUT_BLOCK):
        cols = slice(n, n + OUT_BLOCK)
        o_ref[0, :, cols] = o_ref[0, :, cols] + _dot(hmid, wd_ref[:, cols])


def _ffn(h, g, wup_bf, conv_w, conv_b, wdown_bf, tm, nf):
    B, S, D = h.shape
    tf = D_FF // nf
    return pl.pallas_call(
        functools.partial(_ffn_kernel, tm=tm),
        grid=(B, S // tm, nf),
        in_specs=[pl.BlockSpec((1, tm, D), lambda b, i, j: (b, i, 0)),
                  pl.BlockSpec((1, D), lambda b, i, j: (0, 0)),
                  pl.BlockSpec((D, tf), lambda b, i, j: (0, j)),
                  pl.BlockSpec((D, tf), lambda b, i, j: (0, j + nf)),
                  pl.BlockSpec((CONV_W, tf), lambda b, i, j: (0, j)),
                  pl.BlockSpec((1, tf), lambda b, i, j: (0, j)),
                  pl.BlockSpec((tf, D), lambda b, i, j: (j, 0))],
        out_specs=pl.BlockSpec((1, tm, D), lambda b, i, j: (b, i, 0)),
        out_shape=jax.ShapeDtypeStruct((B, S, D), F32),
        scratch_shapes=[pltpu.VMEM((tm, D), BF16),
                        pltpu.VMEM((V7X_SUBLANES + tm, tf), F32),
                        pltpu.VMEM((nf, V7X_SUBLANES, tf), F32)],
        compiler_params=_cparams(("parallel", "arbitrary", "arbitrary")),
        name="ffn",
    )(h, g, wup_bf, wup_bf, conv_w, conv_b, wdown_bf)


def _tiles(S):
    pick = lambda pref: max(c for c in (64, 128, 256, 512, 1024) if c <= pref and S % c == 0)
    assert S % ATTN_T == 0
    return dict(mix=pick(512), rwkv=pick(512), mid=pick(512),
                ffn=pick(512))


def _block_diag(blocks):
    n = len(blocks)
    rows = []
    for i, blk in enumerate(blocks):
        rows.append(jnp.concatenate(
            [blk if j == i else jnp.zeros_like(blk) for j in range(n)], axis=1))
    return jnp.concatenate(rows, axis=0)


def kernel(x, mem, mix_norm_g, w_in, pool_w, pool_scale, rwkv_mu, rwkv_w0, rwkv_w2, rwkv_a0, rwkv_a2, rwkv_g2, rwkv_k_k, rwkv_k_a, rwkv_r_k, rwkv_ln_w, rwkv_ln_b, diff_q_norm, diff_k_norm, diff_lq1, diff_lk1, diff_lq2, diff_lk2, diff_subln, w_out, xa_norm_g, mem_norm_g, xa_wq, xa_wk, xa_wv, xa_wo, xa_q_norm, xa_k_norm, ffn_norm_g, ffn_w_up, ffn_conv_w, ffn_conv_b, ffn_w_down):
    B, S, D = x.shape
    depth = w_in.shape[0]
    tl = _tiles(S)
    row = lambda a: a.reshape(1, -1).astype(F32)
    h = x
    for l in range(depth):
        lambda_init = 0.8 - 0.6 * math.exp(-0.3 * l)
        qgain = row(jnp.tile(diff_q_norm[l].reshape(-1), DIFF_HEADS)) * (DIFF_QK ** -0.5 * LOG2E)
        kgain = row(jnp.tile(diff_k_norm[l].reshape(-1), DIFF_HEADS))
        zeros64 = jnp.zeros((64, RWKV_WIDTH), F32)
        w2p = jnp.concatenate([rwkv_w2[l], zeros64], axis=0).astype(BF16)
        a2p = jnp.concatenate([zeros64, rwkv_a2[l]], axis=0).astype(BF16)
        pool_bd = _block_diag([pool_w[l, gi] for gi in range(len(POOL_WINDOWS))]).astype(BF16)

        y_pool, z_rwkv, qd, kd, vd = _mix_in(h, row(mix_norm_g[l]), w_in[l].astype(BF16),
                                             qgain, kgain, pool_bd, row(pool_scale[l]), tl["mix"])
        y_rwkv = _rwkv(z_rwkv, row(rwkv_mu[l]), row(rwkv_w0[l]), w2p, row(rwkv_a0[l]), a2p,
                       rwkv_g2[l].astype(BF16), row(rwkv_k_k[l]), row(rwkv_k_a[l]),
                       row(rwkv_r_k[l]), row(rwkv_ln_w[l]), row(rwkv_ln_b[l]), tl["rwkv"])
        y_diff = _diffattn(qd, kd, vd, row(diff_lq1[l]), row(diff_lk1[l]), row(diff_lq2[l]),
                           row(diff_lk2[l]), row(diff_subln[l]), lambda_init)
        kmem, vmem = _memkv(mem, row(mem_norm_g[l]), xa_wk[l].astype(BF16),
                            xa_wv[l].astype(BF16), row(xa_k_norm[l]))
        h = _mid(h, y_pool, y_rwkv, y_diff, w_out[l].astype(BF16), row(xa_norm_g[l]),
                 xa_wq[l].astype(BF16), kmem, vmem, xa_wo[l].astype(BF16), row(xa_q_norm[l]),
                 tl["mid"])
        h = _ffn(h, row(ffn_norm_g[l]), ffn_w_up[l].astype(BF16), ffn_conv_w[l].astype(F32),
                 row(ffn_conv_b[l]), ffn_w_down[l].astype(BF16), tl["ffn"], 2)
    return h
```

```python
import functools
import math

import jax
import jax.numpy as jnp
import numpy as np
from jax import lax
from jax.experimental import pallas as pl
from jax.experimental.pallas import tpu as pltpu

F32 = jnp.float32
BF16 = jnp.bfloat16

D_MODEL = 1024
EPS = 1e-6
CHUNK = 64

POOL_WIDTH = 256
POOL_GDIM = 64
POOL_WINDOWS = (2, 4, 8, 16)
POOL_HALO = 16

RWKV_WIDTH = 256
RWKV_HEAD = 64
RWKV_COLS = 1024
RWKV_GN_EPS = 64e-5
RWKV_CHUNK = 64

DIFF_WIDTH = 512
DIFF_HEADS = 4
DIFF_VDIM = 128
DIFF_QK = 64
P_IN = POOL_WIDTH + RWKV_COLS + 3 * DIFF_WIDTH

XA_HEADS = 4
XA_HEAD = 256
D_FF = 2816
CONV_W = 3
OUT_BLOCK = 256

LOG2E = math.log2(math.e)
NEG_BIG = -1e30
ATTN_T = 512

V7X_SUBLANES = 8
V7X_VMEM_LIMIT = 52 * 1024 * 1024


def _cparams(sem):
    return pltpu.CompilerParams(dimension_semantics=sem, vmem_limit_bytes=V7X_VMEM_LIMIT)


def _dot(a, b):
    return jnp.dot(a, b, preferred_element_type=F32)


def _dot_nt(a, b):
    return lax.dot_general(a, b, (((1,), (1,)), ((), ())), preferred_element_type=F32)


def _dot_tn(a, b):
    return lax.dot_general(a, b, (((0,), (0,)), ((), ())), preferred_element_type=F32)


def _group_ones(width, group):
    idx = np.arange(width) // group
    return jnp.asarray((idx[:, None] == idx[None, :]).astype(np.float32), dtype=BF16)


def _gsum1(x, ones_bd):
    w = ones_bd.shape[0]
    parts = [_dot(x[:, i:i + w].astype(BF16), ones_bd) for i in range(0, x.shape[1], w)]
    return parts[0] if len(parts) == 1 else jnp.concatenate(parts, axis=1)


def _gsum2(x, ones_bd):
    hi = x.astype(BF16)
    lo = (x - hi.astype(F32)).astype(BF16)
    return _dot(hi, ones_bd) + _dot(lo, ones_bd)


def _pool_mixer(u, halo, buf_ref, t0, w_bd, scale):
    tm = u.shape[0]
    pad = V7X_SUBLANES
    n = tm + POOL_HALO
    buf_ref[0:pad, :] = jnp.zeros((pad, POOL_WIDTH), F32)
    buf_ref[pad:pad + POOL_HALO, :] = halo
    buf_ref[pad + POOL_HALO:pad + n, :] = u
    lane = lax.broadcasted_iota(jnp.int32, (tm, POOL_WIDTH), 1)
    grp = lane // POOL_GDIM
    win = jnp.zeros((tm, POOL_WIDTH), F32)
    shift = 1
    for gi, w in enumerate(POOL_WINDOWS):
        while shift < w:
            cur = buf_ref[pad:pad + n, :] + buf_ref[pad - shift:pad - shift + n, :]
            buf_ref[pad:pad + n, :] = cur
            shift *= 2
        win = jnp.where(grp == gi, buf_ref[pad + POOL_HALO:pad + n, :], win)
    t = t0 + lax.broadcasted_iota(jnp.int32, (tm, POOL_WIDTH), 0)
    wlane = jnp.left_shift(2, grp)
    count = jnp.minimum(t + 1, wlane).astype(F32)
    d = win / count - u
    return _dot(d.astype(BF16), w_bd) * scale


def _mix_in_kernel(x_ref, g_ref, w_ref, ones_ref, qg_ref, kg_ref, pw_ref, ps_ref,
                   yp_ref, zr_ref, q_ref, k_ref, v_ref, pbuf, phalo, *, tm):
    i = pl.program_id(1)
    x = x_ref[0]
    ms = jnp.mean(x * x, axis=-1, keepdims=True)
    xn = (x * lax.rsqrt(ms + EPS) * g_ref[...]).astype(BF16)
    z = _dot(xn, w_ref[...])
    u = z[:, :POOL_WIDTH]
    halo = jnp.where(i == 0, 0.0, phalo[...])
    phalo[...] = u[tm - POOL_HALO:tm, :]
    yp_ref[0] = _pool_mixer(u, halo, pbuf, i * tm, pw_ref[...], ps_ref[...]).astype(BF16)
    zr_ref[0] = z[:, POOL_WIDTH:POOL_WIDTH + RWKV_COLS]
    o = POOL_WIDTH + RWKV_COLS
    q = z[:, o:o + DIFF_WIDTH]
    k = z[:, o + DIFF_WIDTH:o + 2 * DIFF_WIDTH]
    v = z[:, o + 2 * DIFF_WIDTH:o + 3 * DIFF_WIDTH]
    ones_bd = ones_ref[...]
    qss = _gsum1(q * q, ones_bd) * (1.0 / DIFF_QK)
    kss = _gsum1(k * k, ones_bd) * (1.0 / DIFF_QK)
    qn = q * lax.rsqrt(qss + EPS) * qg_ref[...]
    k_ref[0] = (k * lax.rsqrt(kss + EPS) * kg_ref[...]).astype(BF16)
    for n in range(tm // ATTN_T):
        rows = slice(n * ATTN_T, (n + 1) * ATTN_T)
        q_ref[0, n] = qn[rows, :].T.astype(BF16)
        v_ref[0, n] = v[rows, :].T.astype(BF16)


def _mix_in(h, g, w_in_bf, qgain, kgain, pool_bd_bf, pool_scale, tm):
    B, S, D = h.shape
    ones_bd = _group_ones(256, DIFF_QK)
    const = lambda shape: pl.BlockSpec(shape, lambda b, i: (0,) * len(shape))
    tile = lambda c: pl.BlockSpec((1, tm, c), lambda b, i: (b, i, 0))
    tile_t = pl.BlockSpec((1, tm // ATTN_T, DIFF_WIDTH, ATTN_T), lambda b, i: (b, i, 0, 0))
    shape_t = jax.ShapeDtypeStruct((B, S // ATTN_T, DIFF_WIDTH, ATTN_T), BF16)
    return pl.pallas_call(
        functools.partial(_mix_in_kernel, tm=tm),
        grid=(B, S // tm),
        in_specs=[tile(D), const((1, D)), const((D, P_IN)), const((256, 256)),
                  const((1, DIFF_WIDTH)), const((1, DIFF_WIDTH)),
                  const((POOL_WIDTH, POOL_WIDTH)), const((1, POOL_WIDTH))],
        out_specs=[tile(POOL_WIDTH), tile(RWKV_COLS), tile_t, tile(DIFF_WIDTH), tile_t],
        out_shape=[jax.ShapeDtypeStruct((B, S, POOL_WIDTH), BF16),
                   jax.ShapeDtypeStruct((B, S, RWKV_COLS), F32),
                   shape_t,
                   jax.ShapeDtypeStruct((B, S, DIFF_WIDTH), BF16),
                   shape_t],
        scratch_shapes=[pltpu.VMEM((V7X_SUBLANES + POOL_HALO + tm, POOL_WIDTH), F32),
                        pltpu.VMEM((POOL_HALO, POOL_WIDTH), F32)],
        compiler_params=_cparams(("parallel", "arbitrary")),
        name="mix_in",
    )(h, g, w_in_bf, ones_bd, qgain, kgain, pool_bd_bf, pool_scale)


RWKV_PAIR = 2 * RWKV_HEAD


def _bd2(x, lo_mask):
    x = x.astype(BF16)
    zero = jnp.zeros_like(x)
    return jnp.concatenate([jnp.where(lo_mask, x, zero), jnp.where(lo_mask, zero, x)], axis=0)


def _diag_blocks(full, lo_mask):
    n = full.shape[1] // RWKV_PAIR
    lo = jnp.concatenate([lo_mask] * n, axis=1) if n > 1 else lo_mask
    return jnp.where(lo, full[0:RWKV_HEAD], full[RWKV_HEAD:2 * RWKV_HEAD])


def _rwkv_kernel(z_ref, halo_ref, mu_ref, w0_ref, w2_ref, a0_ref, a2_ref, g2_ref,
                 kk_ref, ka_ref, rk_ref, lnw_ref, lnb_ref, ones_ref, tri_ref,
                 y_ref,
                 zbuf, r_s, k_s, v_s, lw_s, a_s, b_s, y_s, state, rh_s, g_s, ml_s, gate_s, lw3_s,
                 *, tm):
    i = pl.program_id(1)
    L = RWKV_CHUNK
    W = RWKV_WIDTH
    pad = V7X_SUBLANES

    @pl.when(i == 0)
    def _():
        state[...] = jnp.zeros_like(state)

    zbuf[0:pad, :] = jnp.where(i == 0, 0.0, halo_ref[0])
    zbuf[pad:pad + tm, :] = z_ref[0]
    ones_bd = ones_ref[...]

    def tokenwise(r0, r1):
        z = zbuf[pad + r0:pad + r1, :]
        zprev = zbuf[pad - 1 + r0:pad - 1 + r1, :]
        zm = z + mu_ref[...] * (zprev - z)
        yield
        r = zm[:, 0:W]
        k = zm[:, W:2 * W]
        z6 = zm[:, 3 * W:3 * W + 128]
        gd = zm[:, 3 * W + 128:3 * W + 256]
        r_s[r0:r1, :] = r
        v_s[r0:r1, :] = zm[:, 2 * W:3 * W]
        wl = w0_ref[...] + _dot(jnp.tanh(z6).astype(BF16), w2_ref[...])
        w = -jax.nn.softplus(-wl) - 0.5
        yield
        a = jax.nn.sigmoid(a0_ref[...] + _dot(z6.astype(BF16), a2_ref[...]))
        gate_s[r0:r1, :] = _dot(jax.nn.sigmoid(gd).astype(BF16), g2_ref[...])
        yield
        kk = k * kk_ref[...]
        kk = kk * lax.rsqrt(jnp.maximum(_gsum1(kk * kk, ones_bd), 1e-24))
        k_s[r0:r1, :] = k * (1.0 + (a - 1.0) * ka_ref[...])
        a_s[r0:r1, :] = -kk
        b_s[r0:r1, :] = kk * a
        yield
        rest = -jnp.exp(w)
        lw_s[r0:r1, :] = rest
        for n in range(3):
            term = rest.astype(BF16)
            lw3_s[n, r0:r1, :] = term
            rest = rest - term.astype(F32)
        yield

    PW = RWKV_PAIR
    n_pairs = W // PW
    lane2 = lax.broadcasted_iota(jnp.int32, (L, PW), 1)
    trow = lax.broadcasted_iota(jnp.int32, (L, PW), 0)
    lo_mask = lane2 < RWKV_HEAD
    jcol = jnp.bitwise_and(lane2, RWKV_HEAD - 1)
    strict = trow > jcol
    incl = trow >= jcol
    eye = trow == jcol
    bf = lambda x: x.astype(BF16)
    rows2 = lambda x, y: jnp.concatenate([x, y], axis=0)

    def local_stages(chunk_ids):
        chains = [(ci, pg) for ci in chunk_ids for pg in range(n_pairs)]
        pre = {}
        for ci in chunk_ids:
            sl = slice(ci * L, (ci + 1) * L)
            lw = lw_s[sl, :]
            c_in = (_dot(tri_ref[...], lw3_s[0, sl, :]) + _dot(tri_ref[...], lw3_s[1, sl, :])
                    + _dot(tri_ref[...], lw3_s[2, sl, :]))
            c_tot = c_in[L - 1:L, :]
            e_neg = jnp.exp(-c_in)
            e_rem = jnp.exp(c_tot - c_in)
            pre[ci] = dict(
                at=a_s[sl, :] * jnp.exp(c_in - lw), rt=r_s[sl, :] * jnp.exp(c_in),
                bt=b_s[sl, :] * e_neg, kt=k_s[sl, :] * e_neg,
                bbar=b_s[sl, :] * e_rem, kbar=k_s[sl, :] * e_rem,
                v=v_s[sl, :], e_tot=jnp.exp(c_tot))
        pair = lambda ci, pg, name: pre[ci][name][:, pg * PW:(pg + 1) * PW]
        yield
        st = []
        for ci, pg in chains:
            at, rt = pair(ci, pg, "at"), pair(ci, pg, "rt")
            lhs = bf(rows2(at, rt))
            tb = _dot_nt(lhs, _bd2(pair(ci, pg, "bt"), lo_mask))
            tk = _dot_nt(lhs, _bd2(pair(ci, pg, "kt"), lo_mask))
            tab = jnp.where(strict, tb[0:L], 0.0)
            st.append(dict(
                at=at, rt=rt, v=pair(ci, pg, "v"), tab=tab,
                trb=bf(jnp.where(incl, tb[L:2 * L], 0.0)),
                tak=jnp.where(strict, tk[0:L], 0.0), trk=jnp.where(incl, tk[L:2 * L], 0.0),
                winv=jnp.where(eye, 1.0, 0.0) + tab))
        yield
        for c in st:
            c["p"] = _dot(bf(c["tab"]), _bd2(c["tab"], lo_mask))
        yield
        for _ in range(int(math.log2(L)) - 2):
            for c in st:
                res = _dot(bf(rows2(c["p"], c["winv"])), _bd2(c["p"], lo_mask))
                c["p"] = res[0:L]
                c["winv"] = c["winv"] + res[L:2 * L]
            yield
        for c in st:
            c["winv"] = c["winv"] + _dot(bf(c["winv"]), _bd2(c["p"], lo_mask))
            c["tv"] = _dot(bf(rows2(c["tak"], c["trk"])), _bd2(c["v"], lo_mask))
        yield
        for c in st:
            x_bd = jnp.concatenate([_bd2(c["at"], lo_mask), _bd2(c["tv"][0:L], lo_mask)], axis=1)
            c["wx"] = _dot(bf(c["winv"]), x_bd)
        yield
        for c in st:
            wx = c["wx"]
            ax_bd = jnp.concatenate([_bd2(wx[:, 0:PW], lo_mask), _bd2(wx[:, PW:2 * PW], lo_mask)],
                                    axis=1)
            zz = _dot(c["trb"], ax_bd)
            c["rh"] = c["rt"] + zz[:, 0:PW]
            c["yl"] = zz[:, PW:2 * PW] + c["tv"][L:2 * L]
        yield "last stage next"
        for n, ((ci, pg), c) in enumerate(zip(chains, st)):
            if n and n % n_pairs == 0:
                yield
            v = c["v"]
            lhs_t = bf(rows2(pair(ci, pg, "bbar"), pair(ci, pg, "kbar")))
            rhs_t = bf(rows2(c["wx"], jnp.concatenate([jnp.zeros_like(v), v], axis=1)))
            full = _diag_blocks(_dot_tn(lhs_t, rhs_t), lo_mask)
            gs = slice(pg * PW, (pg + 1) * PW)
            sl = slice(ci * L, (ci + 1) * L)
            rh_s[sl, gs] = bf(c["rh"])
            y_s[sl, gs] = c["yl"]
            g_s[ci, :, gs] = bf(full[:, 0:PW] + jnp.where(eye, pre[ci]["e_tot"][:, gs], 0.0))
            ml_s[ci, :, gs] = full[:, PW:2 * PW]
        yield

    m_cur = [state[:, g * PW:(g + 1) * PW] for g in range(n_pairs)]

    def state_steps(chunk_ids):
        for ci in chunk_ids:
            sl = slice(ci * L, (ci + 1) * L)
            for pg in range(n_pairs):
                gs = slice(pg * PW, (pg + 1) * PW)
                res = _dot(rows2(rh_s[sl, gs], g_s[ci, :, gs]), _bd2(m_cur[pg], lo_mask))
                y_s[sl, gs] = y_s[sl, gs] + res[0:L]
                m_cur[pg] = res[L:2 * L] + ml_s[ci, :, gs]
            yield

    def emit(*gens):
        live = list(gens)
        while live:
            for gen in list(live):
                if next(gen, "done") == "done":
                    live.remove(gen)

    chunks = list(range(tm // L))
    emit(tokenwise(0, tm))
    stages = local_stages(chunks)
    for tag in stages:
        if tag == "last stage next":
            break
    emit(stages, state_steps(chunks))
    for pg in range(n_pairs):
        state[:, pg * PW:(pg + 1) * PW] = m_cur[pg]

    y = y_s[...]
    r = r_s[...]
    kp = k_s[...]
    v = v_s[...]
    inv_n = 1.0 / RWKV_HEAD
    mean = _gsum2(y, ones_bd) * inv_n
    yc = y - mean
    var = _gsum1(yc * yc, ones_bd) * inv_n
    yn = yc * lax.rsqrt(var + RWKV_GN_EPS) * lnw_ref[...] + lnb_ref[...]
    bonus = _gsum1(r * kp * rk_ref[...], ones_bd) * v
    y_ref[0] = ((yn + bonus) * gate_s[...]).astype(BF16)


def _rwkv(z_rwkv, mu, w0, w2p, a0, a2p, g2, k_k, k_a, r_k, ln_w, ln_b, tm):
    B, S, C = z_rwkv.shape
    W = RWKV_WIDTH
    L = RWKV_CHUNK
    assert tm % (2 * L) == 0
    ones_bd = _group_ones(W, RWKV_HEAD)
    tri = jnp.asarray((np.arange(L)[:, None] >= np.arange(L)[None, :]).astype(np.float32),
                      dtype=BF16)
    r = tm // V7X_SUBLANES
    const = lambda shape: pl.BlockSpec(shape, lambda b, i: (0,) * len(shape))
    vec = const((1, W))
    sq = const((W, W))
    return pl.pallas_call(
        functools.partial(_rwkv_kernel, tm=tm),
        grid=(B, S // tm),
        in_specs=[pl.BlockSpec((1, tm, C), lambda b, i: (b, i, 0)),
                  pl.BlockSpec((1, V7X_SUBLANES, C),
                               lambda b, i: (b, jnp.maximum(i * r - 1, 0), 0)),
                  const((1, C)), vec, const((128, W)), vec, const((128, W)), const((128, W)),
                  vec, vec, vec, vec, vec, sq, const((L, L))],
        out_specs=pl.BlockSpec((1, tm, W), lambda b, i: (b, i, 0)),
        out_shape=jax.ShapeDtypeStruct((B, S, W), BF16),
        scratch_shapes=[pltpu.VMEM((V7X_SUBLANES + tm, C), F32)]
        + [pltpu.VMEM((tm, W), F32) for _ in range(7)]
        + [pltpu.VMEM((RWKV_HEAD, W), F32),
           pltpu.VMEM((tm, W), BF16),
           pltpu.VMEM((tm // L, RWKV_HEAD, W), BF16),
           pltpu.VMEM((tm // L, RWKV_HEAD, W), F32),
           pltpu.VMEM((tm, W), F32),
           pltpu.VMEM((3, tm, W), BF16)],
        compiler_params=_cparams(("parallel", "arbitrary")),
        name="rwkv",
    )(z_rwkv, z_rwkv, mu, w0, w2p, a0, a2p, g2, k_k, k_a, r_k, ln_w, ln_b, ones_bd, tri)


DIFF_ONES_ROWS = 16
DIFF_HPS = 2


def _diff_kernel(qt_ref, k_ref, vt_ref, lq1_ref, lk1_ref, lq2_ref, lk2_ref, subln_ref,
                 o_ref, bias2, m_s, acc_s, s_a, s_b, mx_a, mx_b, *, lambda_init):
    t = ATTN_T
    hw = 2 * DIFF_QK
    hp = pl.program_id(1)
    i = pl.program_id(2)
    slopes = []
    for hh in range(DIFF_HPS):
        s2 = jnp.float32(0.0)
        for hd in range(DIFF_HEADS):
            s2 = jnp.where(hp * DIFF_HPS + hh == hd,
                           2.0 ** (-8.0 * (hd + 1) / DIFF_HEADS) * LOG2E, s2)
        slopes.append(s2)

    @pl.when(i == 0)
    def _():
        kc = lax.broadcasted_iota(jnp.int32, (t, t), 0)
        qr = lax.broadcasted_iota(jnp.int32, (t, t), 1)
        vis = (kc // CHUNK) <= (qr // CHUNK)
        rel = (qr - jnp.abs(qr - kc)).astype(F32)
        for hh in range(DIFF_HPS):
            bias2[hh, 0] = slopes[hh] * kc.astype(F32)
            bias2[hh, 1] = jnp.where(vis, slopes[hh] * rel, NEG_BIG)

    row = lax.broadcasted_iota(jnp.int32, (hw, t), 0)
    qc = []
    for hh in range(DIFF_HPS):
        qt = qt_ref[0, 0, hh * hw:(hh + 1) * hw, :]
        qc.append(jnp.where(row < DIFF_QK, qt, jnp.zeros_like(qt)))
        qc.append(jnp.where(row >= DIFF_QK, qt, jnp.zeros_like(qt)))
    ones = jnp.ones((DIFF_ONES_ROWS, t), BF16)

    m_s[...] = jnp.full(m_s.shape, NEG_BIG, F32)
    acc_s[...] = jnp.zeros_like(acc_s)

    n_chains = 2 * DIFF_HPS

    def scores(n, j, buf, mx):
        hh = n // 2
        sl = pl.ds(pl.multiple_of(j * t, t), t)
        kt = k_ref[0, sl, hh * hw:(hh + 1) * hw]
        sb = _dot(kt, qc[n]) + bias2[hh, (j == i).astype(jnp.int32)]
        buf[n] = sb
        mx[n] = jnp.max(sb, axis=0, keepdims=True)

    def consume(n, j, buf, mx, diagonal=False):
        hh = n // 2
        cj = slopes[hh] * ((j - i) * t).astype(F32)
        vaug = jnp.concatenate([vt_ref[0, j, hh * DIFF_VDIM:(hh + 1) * DIFF_VDIM, :], ones],
                               axis=0)
        m_old = m_s[n]
        m_new = jnp.maximum(m_old, mx[n] + cj)
        alpha = jnp.exp2(m_old - m_new)
        shift = m_new - cj
        if not diagonal:
            upd = _dot(vaug, jnp.exp2(buf[n] - shift).astype(BF16))
        else:
            h2 = t // 2
            p_top = jnp.exp2(buf[n, 0:h2, :] - shift)
            p_bot = jnp.exp2(buf[n, h2:t, h2:t] - shift[:, h2:t])
            top = _dot(vaug[:, 0:h2], p_top.astype(BF16))
            bot = _dot(vaug[:, h2:t], p_bot.astype(BF16))
            upd = jnp.concatenate([top[:, 0:h2], top[:, h2:t] + bot], axis=1)
        acc_s[n] = alpha * acc_s[n] + upd
        m_s[n] = m_new

    def step(j_next, nxt, j_cur, cur):
        for n in range(n_chains):
            if nxt is not None:
                scores(n, j_next, *nxt)
            if cur is not None:
                consume(n, j_cur, *cur)

    buf_a, buf_b = (s_a, mx_a), (s_b, mx_b)
    step(0, buf_a, None, None)

    def pair(pp, carry):
        j = 2 * pp
        step(j + 1, buf_b, j, buf_a)
        step(j + 2, buf_a, j + 1, buf_b)
        return carry

    lax.fori_loop(0, i // 2, pair, 0)

    lam = (jnp.exp(jnp.sum(lq1_ref[...] * lk1_ref[...], axis=-1, keepdims=True))
           - jnp.exp(jnp.sum(lq2_ref[...] * lk2_ref[...], axis=-1, keepdims=True))
           + lambda_init)
    dv = DIFF_VDIM

    def last_tile(buf):
        for hh in range(DIFF_HPS):
            a0, a1 = 2 * hh, 2 * hh + 1
            consume(a0, i, *buf, diagonal=True)
            consume(a1, i, *buf, diagonal=True)
            ot = (acc_s[a0, 0:dv, :] / acc_s[a0, dv:dv + 1, :]
                  - lam * (acc_s[a1, 0:dv, :] / acc_s[a1, dv:dv + 1, :]))
            o = ot.T
            ms = jnp.mean(o * o, axis=-1, keepdims=True)
            o = o * lax.rsqrt(ms + EPS) * subln_ref[...] * (1.0 - lambda_init)
            o_ref[0, :, hh * dv:(hh + 1) * dv] = o.astype(BF16)

    @pl.when(i % 2 == 0)
    def _():
        last_tile(buf_a)

    @pl.when(i % 2 == 1)
    def _():
        step(i, buf_b, i - 1, buf_a)
        last_tile(buf_b)


def _diffattn(qt, k, vt, lq1, lk1, lq2, lk2, subln, lambda_init):
    B, S, _ = k.shape
    t = ATTN_T
    n = DIFF_HPS
    hw = 2 * DIFF_QK
    nch = 2 * n
    vec = pl.BlockSpec((1, DIFF_QK), lambda b, h, i: (0, 0))
    return pl.pallas_call(
        functools.partial(_diff_kernel, lambda_init=lambda_init),
        grid=(B, DIFF_HEADS // n, S // t),
        in_specs=[pl.BlockSpec((1, 1, n * hw, t), lambda b, h, i: (b, i, h, 0)),
                  pl.BlockSpec((1, S, n * hw), lambda b, h, i: (b, 0, h)),
                  pl.BlockSpec((1, S // t, n * DIFF_VDIM, t), lambda b, h, i: (b, 0, h, 0)),
                  vec, vec, vec, vec,
                  pl.BlockSpec((1, DIFF_VDIM), lambda b, h, i: (0, 0))],
        out_specs=pl.BlockSpec((1, t, n * DIFF_VDIM), lambda b, h, i: (b, i, h)),
        out_shape=jax.ShapeDtypeStruct((B, S, DIFF_WIDTH), BF16),
        scratch_shapes=[pltpu.VMEM((n, 2, t, t), F32),
                        pltpu.VMEM((nch, 1, t), F32),
                        pltpu.VMEM((nch, DIFF_VDIM + DIFF_ONES_ROWS, t), F32),
                        pltpu.VMEM((nch, t, t), F32),
                        pltpu.VMEM((nch, t, t), F32),
                        pltpu.VMEM((nch, 1, t), F32),
                        pltpu.VMEM((nch, 1, t), F32)],
        compiler_params=_cparams(("parallel", "arbitrary", "arbitrary")),
        name="diffattn",
    )(qt, k, vt, lq1, lk1, lq2, lk2, subln)


def _memkv_kernel(mem_ref, g_ref, wk_ref, wv_ref, kn_ref, k_ref, v_ref):
    x = mem_ref[0]
    ms = jnp.mean(x * x, axis=-1, keepdims=True)
    xn = (x * lax.rsqrt(ms + EPS) * g_ref[...]).astype(BF16)
    k = _dot(xn, wk_ref[...])
    v = _dot(xn, wv_ref[...])
    ks = []
    for h in range(XA_HEADS):
        kh = k[:, h * XA_HEAD:(h + 1) * XA_HEAD]
        kms = jnp.mean(kh * kh, axis=-1, keepdims=True)
        ks.append(kh * lax.rsqrt(kms + EPS) * kn_ref[...])
    k_ref[0] = jnp.concatenate(ks, axis=1).astype(BF16)
    v_ref[0] = v.astype(BF16)


def _memkv(mem, g, wk_bf, wv_bf, k_norm):
    B, M, D = mem.shape
    const = lambda shape: pl.BlockSpec(shape, lambda b: (0,) * len(shape))
    tile = pl.BlockSpec((1, M, D), lambda b: (b, 0, 0))
    return pl.pallas_call(
        _memkv_kernel,
        grid=(B,),
        in_specs=[tile, const((1, D)), const((D, D)), const((D, D)), const((1, XA_HEAD))],
        out_specs=[tile, tile],
        out_shape=[jax.ShapeDtypeStruct((B, M, D), BF16), jax.ShapeDtypeStruct((B, M, D), BF16)],
        compiler_params=_cparams(("parallel",)),
        name="memkv",
    )(mem, g, wk_bf, wv_bf, k_norm)


def _mid_kernel(h_ref, yp_ref, yr_ref, yd_ref, wout_ref, g_ref, wq_ref, km_ref, vm_ref,
                wo_ref, qn_ref, o_ref):
    a = POOL_WIDTH
    b = POOL_WIDTH + RWKV_WIDTH
    h1 = (h_ref[0] + _dot(yp_ref[0], wout_ref[0:a, :]) + _dot(yr_ref[0], wout_ref[a:b, :])
          + _dot(yd_ref[0], wout_ref[b:D_MODEL, :]))
    ms = jnp.mean(h1 * h1, axis=-1, keepdims=True)
    xn = (h1 * lax.rsqrt(ms + EPS) * g_ref[...]).astype(BF16)
    q = _dot(xn, wq_ref[...])
    qscale = (XA_HEAD ** -0.5) * LOG2E
    heads = [slice(hd * XA_HEAD, (hd + 1) * XA_HEAD) for hd in range(XA_HEADS)]
    scores = []
    for sl in heads:
        qh = q[:, sl]
        qms = jnp.mean(qh * qh, axis=-1, keepdims=True)
        qh = (qh * lax.rsqrt(qms + EPS) * (qn_ref[...] * qscale)).astype(BF16)
        scores.append(_dot_nt(qh, km_ref[0, :, sl]))
    probs, sums = [], []
    for s in scores:
        p = jnp.exp2(s - jnp.max(s, axis=-1, keepdims=True))
        sums.append(jnp.sum(p, axis=-1, keepdims=True))
        probs.append(p.astype(BF16))
    outs = [_dot(p, vm_ref[0, :, sl]) / l for p, l, sl in zip(probs, sums, heads)]
    o = jnp.concatenate(outs, axis=1).astype(BF16)
    for n in range(0, D_MODEL, OUT_BLOCK):
        cols = slice(n, n + OUT_BLOCK)
        o_ref[0, :, cols] = h1[:, cols] + _dot(o, wo_ref[:, cols])


def _mid(h, yp, yr, yd, wout_bf, g, wq_bf, kmem, vmem, wo_bf, q_norm, tm):
    B, S, D = h.shape
    M = kmem.shape[1]
    const = lambda shape: pl.BlockSpec(shape, lambda b, i: (0,) * len(shape))
    tile = lambda c: pl.BlockSpec((1, tm, c), lambda b, i: (b, i, 0))
    memspec = pl.BlockSpec((1, M, D), lambda b, i: (b, 0, 0))
    return pl.pallas_call(
        _mid_kernel,
        grid=(B, S // tm),
        in_specs=[tile(D), tile(POOL_WIDTH), tile(RWKV_WIDTH), tile(DIFF_WIDTH), const((D, D)),
                  const((1, D)), const((D, D)), memspec, memspec, const((D, D)),
                  const((1, XA_HEAD))],
        out_specs=tile(D),
        out_shape=jax.ShapeDtypeStruct((B, S, D), F32),
        compiler_params=_cparams(("parallel", "parallel")),
        name="mid",
    )(h, yp, yr, yd, wout_bf, g, wq_bf, kmem, vmem, wo_bf, q_norm)


def _ffn_kernel(h_ref, g_ref, wa_ref, wb_ref, cw_ref, cb_ref, wd_ref, o_ref,
                xn_s, abuf, carry, *, tm):
    i = pl.program_id(1)
    j = pl.program_id(2)
    pad = V7X_SUBLANES

    @pl.when(j == 0)
    def _():
        x = h_ref[0]
        ms = jnp.mean(x * x, axis=-1, keepdims=True)
        xn_s[...] = (x * lax.rsqrt(ms + EPS) * g_ref[...]).astype(BF16)
        o_ref[0] = x

    xn = xn_s[...]
    a = _dot(xn, wa_ref[...])
    b = _dot(xn, wb_ref[...])
    abuf[0:pad, :] = jnp.where(i == 0, 0.0, carry[j])
    abuf[pad:pad + tm, :] = a
    carry[j] = a[tm - pad:tm, :]
    cw = cw_ref[...]
    c = (cw[2:3, :] * a + cw[1:2, :] * abuf[pad - 1:pad - 1 + tm, :]
         + cw[0:1, :] * abuf[pad - 2:pad - 2 + tm, :] + cb_ref[...])
    gelu = 0.5 * c * (1.0 + lax.erf(c * (2.0 ** -0.5)))
    hmid = (gelu * b).astype(BF16)
    for n in range(0, D_MODEL, OUT_BLOCK):
        cols = slice(n, n + OUT_BLOCK)
        o_ref[0, :, cols] = o_ref[0, :, cols] + _dot(hmid, wd_ref[:, cols])


def _ffn(h, g, wup_bf, conv_w, conv_b, wdown_bf, tm, nf):
    B, S, D = h.shape
    tf = D_FF // nf
    return pl.pallas_call(
        functools.partial(_ffn_kernel, tm=tm),
        grid=(B, S // tm, nf),
        in_specs=[pl.BlockSpec((1, tm, D), lambda b, i, j: (b, i, 0)),
                  pl.BlockSpec((1, D), lambda b, i, j: (0, 0)),
                  pl.BlockSpec((D, tf), lambda b, i, j: (0, j)),
                  pl.BlockSpec((D, tf), lambda b, i, j: (0, j + nf)),
                  pl.BlockSpec((CONV_W, tf), lambda b, i, j: (0, j)),
                  pl.BlockSpec((1, tf), lambda b, i, j: (0, j)),
                  pl.BlockSpec((tf, D), lambda b, i, j: (j, 0))],
        out_specs=pl.BlockSpec((1, tm, D), lambda b, i, j: (b, i, 0)),
        out_shape=jax.ShapeDtypeStruct((B, S, D), F32),
        scratch_shapes=[pltpu.VMEM((tm, D), BF16),
                        pltpu.VMEM((V7X_SUBLANES + tm, tf), F32),
                        pltpu.VMEM((nf, V7X_SUBLANES, tf), F32)],
        compiler_params=_cparams(("parallel", "arbitrary", "arbitrary")),
        name="ffn",
    )(h, g, wup_bf, wup_bf, conv_w, conv_b, wdown_bf)


def _tiles(S):
    pick = lambda pref: max(c for c in (64, 128, 256, 512, 1024) if c <= pref and S % c == 0)
    assert S % ATTN_T == 0
    return dict(mix=pick(512), rwkv=pick(512), mid=pick(512),
                ffn=pick(512))


def _block_diag(blocks):
    n = len(blocks)
    rows = []
    for i, blk in enumerate(blocks):
        rows.append(jnp.concatenate(
            [blk if j == i else jnp.zeros_like(blk) for j in range(n)], axis=1))
    return jnp.concatenate(rows, axis=0)


def kernel(x, mem, mix_norm_g, w_in, pool_w, pool_scale, rwkv_mu, rwkv_w0, rwkv_w2, rwkv_a0, rwkv_a2, rwkv_g2, rwkv_k_k, rwkv_k_a, rwkv_r_k, rwkv_ln_w, rwkv_ln_b, diff_q_norm, diff_k_norm, diff_lq1, diff_lk1, diff_lq2, diff_lk2, diff_subln, w_out, xa_norm_g, mem_norm_g, xa_wq, xa_wk, xa_wv, xa_wo, xa_q_norm, xa_k_norm, ffn_norm_g, ffn_w_up, ffn_conv_w, ffn_conv_b, ffn_w_down):
    B, S, D = x.shape
    depth = w_in.shape[0]
    tl = _tiles(S)
    row = lambda a: a.reshape(1, -1).astype(F32)
    h = x
    for l in range(depth):
        lambda_init = 0.8 - 0.6 * math.exp(-0.3 * l)
        qgain = row(jnp.tile(diff_q_norm[l].reshape(-1), DIFF_HEADS)) * (DIFF_QK ** -0.5 * LOG2E)
        kgain = row(jnp.tile(diff_k_norm[l].reshape(-1), DIFF_HEADS))
        zeros64 = jnp.zeros((64, RWKV_WIDTH), F32)
        w2p = jnp.concatenate([rwkv_w2[l], zeros64], axis=0).astype(BF16)
        a2p = jnp.concatenate([zeros64, rwkv_a2[l]], axis=0).astype(BF16)
        pool_bd = _block_diag([pool_w[l, gi] for gi in range(len(POOL_WINDOWS))]).astype(BF16)

        y_pool, z_rwkv, qd, kd, vd = _mix_in(h, row(mix_norm_g[l]), w_in[l].astype(BF16),
                                             qgain, kgain, pool_bd, row(pool_scale[l]), tl["mix"])
        y_rwkv = _rwkv(z_rwkv, row(rwkv_mu[l]), row(rwkv_w0[l]), w2p, row(rwkv_a0[l]), a2p,
                       rwkv_g2[l].astype(BF16), row(rwkv_k_k[l]), row(rwkv_k_a[l]),
                       row(rwkv_r_k[l]), row(rwkv_ln_w[l]), row(rwkv_ln_b[l]), tl["rwkv"])
        y_diff = _diffattn(qd, kd, vd, row(diff_lq1[l]), row(diff_lk1[l]), row(diff_lq2[l]),
                           row(diff_lk2[l]), row(diff_subln[l]), lambda_init)
        kmem, vmem = _memkv(mem, row(mem_norm_g[l]), xa_wk[l].astype(BF16),
                            xa_wv[l].astype(BF16), row(xa_k_norm[l]))
        h = _mid(h, y_pool, y_rwkv, y_diff, w_out[l].astype(BF16), row(xa_norm_g[l]),
                 xa_wq[l].astype(BF16), kmem, vmem, xa_wo[l].astype(BF16), row(xa_q_norm[l]),
                 tl["mid"])
        h = _ffn(h, row(ffn_norm_g[l]), ffn_w_up[l].astype(BF16), ffn_conv_w[l].astype(F32),
                 row(ffn_conv_b[l]), ffn_w_down[l].astype(BF16), tl["ffn"], 2)
    return h
```

```python
import functools
import math

import jax
import jax.numpy as jnp
import numpy as np
from jax import lax
from jax.experimental import pallas as pl
from jax.experimental.pallas import tpu as pltpu

F32 = jnp.float32
BF16 = jnp.bfloat16

D_MODEL = 1024
EPS = 1e-6
CHUNK = 64

POOL_WIDTH = 256
POOL_GDIM = 64
POOL_WINDOWS = (2, 4, 8, 16)
POOL_HALO = 16

RWKV_WIDTH = 256
RWKV_HEAD = 64
RWKV_COLS = 1024
RWKV_GN_EPS = 64e-5
RWKV_CHUNK = 64

DIFF_WIDTH = 512
DIFF_HEADS = 4
DIFF_VDIM = 128
DIFF_QK = 64
P_IN = POOL_WIDTH + RWKV_COLS + 3 * DIFF_WIDTH

XA_HEADS = 4
XA_HEAD = 256
D_FF = 2816
CONV_W = 3
OUT_BLOCK = 256

LOG2E = math.log2(math.e)
NEG_BIG = -1e30
ATTN_T = 512

V7X_SUBLANES = 8
V7X_VMEM_LIMIT = 52 * 1024 * 1024


def _cparams(sem):
    return pltpu.CompilerParams(dimension_semantics=sem, vmem_limit_bytes=V7X_VMEM_LIMIT)


def _dot(a, b):
    return jnp.dot(a, b, preferred_element_type=F32)


def _dot_nt(a, b):
    return lax.dot_general(a, b, (((1,), (1,)), ((), ())), preferred_element_type=F32)


def _dot_tn(a, b):
    return lax.dot_general(a, b, (((0,), (0,)), ((), ())), preferred_element_type=F32)


def _group_ones(width, group):
    idx = np.arange(width) // group
    return jnp.asarray((idx[:, None] == idx[None, :]).astype(np.float32), dtype=BF16)


def _gsum1(x, ones_bd):
    w = ones_bd.shape[0]
    parts = [_dot(x[:, i:i + w].astype(BF16), ones_bd) for i in range(0, x.shape[1], w)]
    return parts[0] if len(parts) == 1 else jnp.concatenate(parts, axis=1)


def _gsum2(x, ones_bd):
    hi = x.astype(BF16)
    lo = (x - hi.astype(F32)).astype(BF16)
    return _dot(hi, ones_bd) + _dot(lo, ones_bd)


def _pool_mixer(u, halo, buf_ref, t0, w_bd, scale):
    tm = u.shape[0]
    pad = V7X_SUBLANES
    n = tm + POOL_HALO
    buf_ref[0:pad, :] = jnp.zeros((pad, POOL_WIDTH), F32)
    buf_ref[pad:pad + POOL_HALO, :] = halo
    buf_ref[pad + POOL_HALO:pad + n, :] = u
    lane = lax.broadcasted_iota(jnp.int32, (tm, POOL_WIDTH), 1)
    grp = lane // POOL_GDIM
    win = jnp.zeros((tm, POOL_WIDTH), F32)
    shift = 1
    for gi, w in enumerate(POOL_WINDOWS):
        while shift < w:
            cur = buf_ref[pad:pad + n, :] + buf_ref[pad - shift:pad - shift + n, :]
            buf_ref[pad:pad + n, :] = cur
            shift *= 2
        win = jnp.where(grp == gi, buf_ref[pad + POOL_HALO:pad + n, :], win)
    t = t0 + lax.broadcasted_iota(jnp.int32, (tm, POOL_WIDTH), 0)
    wlane = jnp.left_shift(2, grp)
    count = jnp.minimum(t + 1, wlane).astype(F32)
    d = win / count - u
    return _dot(d.astype(BF16), w_bd) * scale


def _mix_in_kernel(x_ref, g_ref, w_ref, ones_ref, qg_ref, kg_ref, pw_ref, ps_ref,
                   yp_ref, zr_ref, q_ref, k_ref, v_ref, pbuf, phalo, *, tm):
    i = pl.program_id(1)
    x = x_ref[0]
    ms = jnp.mean(x * x, axis=-1, keepdims=True)
    xn = (x * lax.rsqrt(ms + EPS) * g_ref[...]).astype(BF16)
    z = _dot(xn, w_ref[...])
    u = z[:, :POOL_WIDTH]
    halo = jnp.where(i == 0, 0.0, phalo[...])
    phalo[...] = u[tm - POOL_HALO:tm, :]
    yp_ref[0] = _pool_mixer(u, halo, pbuf, i * tm, pw_ref[...], ps_ref[...]).astype(BF16)
    zr_ref[0] = z[:, POOL_WIDTH:POOL_WIDTH + RWKV_COLS]
    o = POOL_WIDTH + RWKV_COLS
    q = z[:, o:o + DIFF_WIDTH]
    k = z[:, o + DIFF_WIDTH:o + 2 * DIFF_WIDTH]
    v = z[:, o + 2 * DIFF_WIDTH:o + 3 * DIFF_WIDTH]
    ones_bd = ones_ref[...]
    qss = _gsum1(q * q, ones_bd) * (1.0 / DIFF_QK)
    kss = _gsum1(k * k, ones_bd) * (1.0 / DIFF_QK)
    qn = q * lax.rsqrt(qss + EPS) * qg_ref[...]
    k_ref[0] = (k * lax.rsqrt(kss + EPS) * kg_ref[...]).astype(BF16)
    for n in range(tm // ATTN_T):
        rows = slice(n * ATTN_T, (n + 1) * ATTN_T)
        q_ref[0, n] = qn[rows, :].T.astype(BF16)
        v_ref[0, n] = v[rows, :].T.astype(BF16)


def _mix_in(h, g, w_in_bf, qgain, kgain, pool_bd_bf, pool_scale, tm):
    B, S, D = h.shape
    ones_bd = _group_ones(256, DIFF_QK)
    const = lambda shape: pl.BlockSpec(shape, lambda b, i: (0,) * len(shape),
                                       pipeline_mode=pl.Buffered(1))
    tile = lambda c: pl.BlockSpec((1, tm, c), lambda b, i: (b, i, 0))
    tile_t = pl.BlockSpec((1, tm // ATTN_T, DIFF_WIDTH, ATTN_T), lambda b, i: (b, i, 0, 0))
    shape_t = jax.ShapeDtypeStruct((B, S // ATTN_T, DIFF_WIDTH, ATTN_T), BF16)
    return pl.pallas_call(
        functools.partial(_mix_in_kernel, tm=tm),
        grid=(B, S // tm),
        in_specs=[tile(D), const((1, D)), const((D, P_IN)), const((256, 256)),
                  const((1, DIFF_WIDTH)), const((1, DIFF_WIDTH)),
                  const((POOL_WIDTH, POOL_WIDTH)), const((1, POOL_WIDTH))],
        out_specs=[tile(POOL_WIDTH), tile(RWKV_COLS), tile_t, tile(DIFF_WIDTH), tile_t],
        out_shape=[jax.ShapeDtypeStruct((B, S, POOL_WIDTH), BF16),
                   jax.ShapeDtypeStruct((B, S, RWKV_COLS), F32),
                   shape_t,
                   jax.ShapeDtypeStruct((B, S, DIFF_WIDTH), BF16),
                   shape_t],
        scratch_shapes=[pltpu.VMEM((V7X_SUBLANES + POOL_HALO + tm, POOL_WIDTH), F32),
                        pltpu.VMEM((POOL_HALO, POOL_WIDTH), F32)],
        compiler_params=_cparams(("parallel", "arbitrary")),
        name="mix_in",
    )(h, g, w_in_bf, ones_bd, qgain, kgain, pool_bd_bf, pool_scale)


RWKV_PAIR = 2 * RWKV_HEAD


def _bd2(x, lo_mask):
    x = x.astype(BF16)
    zero = jnp.zeros_like(x)
    return jnp.concatenate([jnp.where(lo_mask, x, zero), jnp.where(lo_mask, zero, x)], axis=0)


def _diag_blocks(full, lo_mask):
    n = full.shape[1] // RWKV_PAIR
    lo = jnp.concatenate([lo_mask] * n, axis=1) if n > 1 else lo_mask
    return jnp.where(lo, full[0:RWKV_HEAD], full[RWKV_HEAD:2 * RWKV_HEAD])


def _rwkv_kernel(z_ref, halo_ref, mu_ref, w0_ref, w2_ref, a0_ref, a2_ref, g2_ref,
                 kk_ref, ka_ref, rk_ref, lnw_ref, lnb_ref, ones_ref, tri_ref,
                 y_ref,
                 zbuf, r_s, k_s, v_s, lw_s, a_s, b_s, y_s, state, rh_s, g_s, ml_s, gate_s, lw3_s,
                 *, tm):
    i = pl.program_id(1)
    L = RWKV_CHUNK
    W = RWKV_WIDTH
    pad = V7X_SUBLANES

    @pl.when(i == 0)
    def _():
        state[...] = jnp.zeros_like(state)

    zbuf[0:pad, :] = jnp.where(i == 0, 0.0, halo_ref[0])
    zbuf[pad:pad + tm, :] = z_ref[0]
    ones_bd = ones_ref[...]

    def tokenwise(r0, r1):
        z = zbuf[pad + r0:pad + r1, :]
        zprev = zbuf[pad - 1 + r0:pad - 1 + r1, :]
        zm = z + mu_ref[...] * (zprev - z)
        yield
        r = zm[:, 0:W]
        k = zm[:, W:2 * W]
        z6 = zm[:, 3 * W:3 * W + 128]
        gd = zm[:, 3 * W + 128:3 * W + 256]
        r_s[r0:r1, :] = r
        v_s[r0:r1, :] = zm[:, 2 * W:3 * W]
        wl = w0_ref[...] + _dot(jnp.tanh(z6).astype(BF16), w2_ref[...])
        w = -jax.nn.softplus(-wl) - 0.5
        yield
        a = jax.nn.sigmoid(a0_ref[...] + _dot(z6.astype(BF16), a2_ref[...]))
        gate_s[r0:r1, :] = _dot(jax.nn.sigmoid(gd).astype(BF16), g2_ref[...])
        yield
        kk = k * kk_ref[...]
        kk = kk * lax.rsqrt(jnp.maximum(_gsum1(kk * kk, ones_bd), 1e-24))
        k_s[r0:r1, :] = k * (1.0 + (a - 1.0) * ka_ref[...])
        a_s[r0:r1, :] = -kk
        b_s[r0:r1, :] = kk * a
        yield
        rest = -jnp.exp(w)
        lw_s[r0:r1, :] = rest
        for n in range(3):
            term = rest.astype(BF16)
            lw3_s[n, r0:r1, :] = term
            rest = rest - term.astype(F32)
        yield

    PW = RWKV_PAIR
    n_pairs = W // PW
    lane2 = lax.broadcasted_iota(jnp.int32, (L, PW), 1)
    trow = lax.broadcasted_iota(jnp.int32, (L, PW), 0)
    lo_mask = lane2 < RWKV_HEAD
    jcol = jnp.bitwise_and(lane2, RWKV_HEAD - 1)
    strict = trow > jcol
    incl = trow >= jcol
    eye = trow == jcol
    bf = lambda x: x.astype(BF16)
    rows2 = lambda x, y: jnp.concatenate([x, y], axis=0)

    def local_stages(chunk_ids):
        chains = [(ci, pg) for ci in chunk_ids for pg in range(n_pairs)]
        pre = {}
        for ci in chunk_ids:
            sl = slice(ci * L, (ci + 1) * L)
            lw = lw_s[sl, :]
            c_in = (_dot(tri_ref[...], lw3_s[0, sl, :]) + _dot(tri_ref[...], lw3_s[1, sl, :])
                    + _dot(tri_ref[...], lw3_s[2, sl, :]))
            c_tot = c_in[L - 1:L, :]
            e_neg = jnp.exp(-c_in)
            e_rem = jnp.exp(c_tot - c_in)
            pre[ci] = dict(
                at=a_s[sl, :] * jnp.exp(c_in - lw), rt=r_s[sl, :] * jnp.exp(c_in),
                bt=b_s[sl, :] * e_neg, kt=k_s[sl, :] * e_neg,
                bbar=b_s[sl, :] * e_rem, kbar=k_s[sl, :] * e_rem,
                v=v_s[sl, :], e_tot=jnp.exp(c_tot))
        pair = lambda ci, pg, name: pre[ci][name][:, pg * PW:(pg + 1) * PW]
        yield
        st = []
        for ci, pg in chains:
            at, rt = pair(ci, pg, "at"), pair(ci, pg, "rt")
            lhs = bf(rows2(at, rt))
            tb = _dot_nt(lhs, _bd2(pair(ci, pg, "bt"), lo_mask))
            tk = _dot_nt(lhs, _bd2(pair(ci, pg, "kt"), lo_mask))
            tab = jnp.where(strict, tb[0:L], 0.0)
            st.append(dict(
                at=at, rt=rt, v=pair(ci, pg, "v"), tab=tab,
                trb=bf(jnp.where(incl, tb[L:2 * L], 0.0)),
                tak=jnp.where(strict, tk[0:L], 0.0), trk=jnp.where(incl, tk[L:2 * L], 0.0),
                winv=jnp.where(eye, 1.0, 0.0) + tab))
        yield
        for c in st:
            c["p"] = _dot(bf(c["tab"]), _bd2(c["tab"], lo_mask))
        yield
        for _ in range(int(math.log2(L)) - 2):
            for c in st:
                res = _dot(bf(rows2(c["p"], c["winv"])), _bd2(c["p"], lo_mask))
                c["p"] = res[0:L]
                c["winv"] = c["winv"] + res[L:2 * L]
            yield
        for c in st:
            c["winv"] = c["winv"] + _dot(bf(c["winv"]), _bd2(c["p"], lo_mask))
            c["tv"] = _dot(bf(rows2(c["tak"], c["trk"])), _bd2(c["v"], lo_mask))
        yield
        for c in st:
            x_bd = jnp.concatenate([_bd2(c["at"], lo_mask), _bd2(c["tv"][0:L], lo_mask)], axis=1)
            c["wx"] = _dot(bf(c["winv"]), x_bd)
        yield
        for c in st:
            wx = c["wx"]
            ax_bd = jnp.concatenate([_bd2(wx[:, 0:PW], lo_mask), _bd2(wx[:, PW:2 * PW], lo_mask)],
                                    axis=1)
            zz = _dot(c["trb"], ax_bd)
            c["rh"] = c["rt"] + zz[:, 0:PW]
            c["yl"] = zz[:, PW:2 * PW] + c["tv"][L:2 * L]
        yield "last stage next"
        for n, ((ci, pg), c) in enumerate(zip(chains, st)):
            if n and n % n_pairs == 0:
                yield
            v = c["v"]
            lhs_t = bf(rows2(pair(ci, pg, "bbar"), pair(ci, pg, "kbar")))
            rhs_t = bf(rows2(c["wx"], jnp.concatenate([jnp.zeros_like(v), v], axis=1)))
            full = _diag_blocks(_dot_tn(lhs_t, rhs_t), lo_mask)
            gs = slice(pg * PW, (pg + 1) * PW)
            sl = slice(ci * L, (ci + 1) * L)
            rh_s[sl, gs] = bf(c["rh"])
            y_s[sl, gs] = c["yl"]
            g_s[ci, :, gs] = bf(full[:, 0:PW] + jnp.where(eye, pre[ci]["e_tot"][:, gs], 0.0))
            ml_s[ci, :, gs] = full[:, PW:2 * PW]
        yield

    m_cur = [state[:, g * PW:(g + 1) * PW] for g in range(n_pairs)]

    def state_steps(chunk_ids):
        for ci in chunk_ids:
            sl = slice(ci * L, (ci + 1) * L)
            for pg in range(n_pairs):
                gs = slice(pg * PW, (pg + 1) * PW)
                res = _dot(rows2(rh_s[sl, gs], g_s[ci, :, gs]), _bd2(m_cur[pg], lo_mask))
                y_s[sl, gs] = y_s[sl, gs] + res[0:L]
                m_cur[pg] = res[L:2 * L] + ml_s[ci, :, gs]
            yield

    def emit(*gens):
        live = list(gens)
        while live:
            for gen in list(live):
                if next(gen, "done") == "done":
                    live.remove(gen)

    chunks = list(range(tm // L))
    emit(tokenwise(0, tm))
    stages = local_stages(chunks)
    for tag in stages:
        if tag == "last stage next":
            break
    emit(stages, state_steps(chunks))
    for pg in range(n_pairs):
        state[:, pg * PW:(pg + 1) * PW] = m_cur[pg]

    y = y_s[...]
    r = r_s[...]
    kp = k_s[...]
    v = v_s[...]
    inv_n = 1.0 / RWKV_HEAD
    mean = _gsum2(y, ones_bd) * inv_n
    yc = y - mean
    var = _gsum1(yc * yc, ones_bd) * inv_n
    yn = yc * lax.rsqrt(var + RWKV_GN_EPS) * lnw_ref[...] + lnb_ref[...]
    bonus = _gsum1(r * kp * rk_ref[...], ones_bd) * v
    y_ref[0] = ((yn + bonus) * gate_s[...]).astype(BF16)


def _rwkv(z_rwkv, mu, w0, w2p, a0, a2p, g2, k_k, k_a, r_k, ln_w, ln_b, tm):
    B, S, C = z_rwkv.shape
    W = RWKV_WIDTH
    L = RWKV_CHUNK
    assert tm % (2 * L) == 0
    ones_bd = _group_ones(W, RWKV_HEAD)
    tri = jnp.asarray((np.arange(L)[:, None] >= np.arange(L)[None, :]).astype(np.float32),
                      dtype=BF16)
    r = tm // V7X_SUBLANES
    const = lambda shape: pl.BlockSpec(shape, lambda b, i: (0,) * len(shape))
    vec = const((1, W))
    sq = const((W, W))
    return pl.pallas_call(
        functools.partial(_rwkv_kernel, tm=tm),
        grid=(B, S // tm),
        in_specs=[pl.BlockSpec((1, tm, C), lambda b, i: (b, i, 0)),
                  pl.BlockSpec((1, V7X_SUBLANES, C),
                               lambda b, i: (b, jnp.maximum(i * r - 1, 0), 0)),
                  const((1, C)), vec, const((128, W)), vec, const((128, W)), const((128, W)),
                  vec, vec, vec, vec, vec, sq, const((L, L))],
        out_specs=pl.BlockSpec((1, tm, W), lambda b, i: (b, i, 0)),
        out_shape=jax.ShapeDtypeStruct((B, S, W), BF16),
        scratch_shapes=[pltpu.VMEM((V7X_SUBLANES + tm, C), F32)]
        + [pltpu.VMEM((tm, W), F32) for _ in range(7)]
        + [pltpu.VMEM((RWKV_HEAD, W), F32),
           pltpu.VMEM((tm, W), BF16),
           pltpu.VMEM((tm // L, RWKV_HEAD, W), BF16),
           pltpu.VMEM((tm // L, RWKV_HEAD, W), F32),
           pltpu.VMEM((tm, W), F32),
           pltpu.VMEM((3, tm, W), BF16)],
        compiler_params=_cparams(("parallel", "arbitrary")),
        name="rwkv",
    )(z_rwkv, z_rwkv, mu, w0, w2p, a0, a2p, g2, k_k, k_a, r_k, ln_w, ln_b, ones_bd, tri)


DIFF_ONES_ROWS = 16
DIFF_HPS = 2


def _diff_kernel(qt_ref, k_ref, vt_ref, lq1_ref, lk1_ref, lq2_ref, lk2_ref, subln_ref,
                 o_ref, bias2, m_s, acc_s, s_a, s_b, mx_a, mx_b, *, lambda_init):
    t = ATTN_T
    hw = 2 * DIFF_QK
    hp = pl.program_id(1)
    i = pl.program_id(2)
    slopes = []
    for hh in range(DIFF_HPS):
        s2 = jnp.float32(0.0)
        for hd in range(DIFF_HEADS):
            s2 = jnp.where(hp * DIFF_HPS + hh == hd,
                           2.0 ** (-8.0 * (hd + 1) / DIFF_HEADS) * LOG2E, s2)
        slopes.append(s2)

    @pl.when(i == 0)
    def _():
        kc = lax.broadcasted_iota(jnp.int32, (t, t), 0)
        qr = lax.broadcasted_iota(jnp.int32, (t, t), 1)
        vis = (kc // CHUNK) <= (qr // CHUNK)
        rel = (qr - jnp.abs(qr - kc)).astype(F32)
        for hh in range(DIFF_HPS):
            bias2[hh, 0] = slopes[hh] * kc.astype(F32)
            bias2[hh, 1] = jnp.where(vis, slopes[hh] * rel, NEG_BIG)

    row = lax.broadcasted_iota(jnp.int32, (hw, t), 0)
    qc = []
    for hh in range(DIFF_HPS):
        qt = qt_ref[0, 0, hh * hw:(hh + 1) * hw, :]
        qc.append(jnp.where(row < DIFF_QK, qt, jnp.zeros_like(qt)))
        qc.append(jnp.where(row >= DIFF_QK, qt, jnp.zeros_like(qt)))
    ones = jnp.ones((DIFF_ONES_ROWS, t), BF16)

    m_s[...] = jnp.full(m_s.shape, NEG_BIG, F32)
    acc_s[...] = jnp.zeros_like(acc_s)

    n_chains = 2 * DIFF_HPS

    def scores(n, j, buf, mx):
        hh = n // 2
        sl = pl.ds(pl.multiple_of(j * t, t), t)
        kt = k_ref[0, sl, hh * hw:(hh + 1) * hw]
        sb = _dot(kt, qc[n]) + bias2[hh, (j == i).astype(jnp.int32)]
        buf[n] = sb
        mx[n] = jnp.max(sb, axis=0, keepdims=True)

    def consume(n, j, buf, mx, diagonal=False):
        hh = n // 2
        cj = slopes[hh] * ((j - i) * t).astype(F32)
        vaug = jnp.concatenate([vt_ref[0, j, hh * DIFF_VDIM:(hh + 1) * DIFF_VDIM, :], ones],
                               axis=0)
        m_old = m_s[n]
        m_new = jnp.maximum(m_old, mx[n] + cj)
        alpha = jnp.exp2(m_old - m_new)
        shift = m_new - cj
        if not diagonal:
            upd = _dot(vaug, jnp.exp2(buf[n] - shift).astype(BF16))
        else:
            h2 = t // 2
            p_top = jnp.exp2(buf[n, 0:h2, :] - shift)
            p_bot = jnp.exp2(buf[n, h2:t, h2:t] - shift[:, h2:t])
            top = _dot(vaug[:, 0:h2], p_top.astype(BF16))
            bot = _dot(vaug[:, h2:t], p_bot.astype(BF16))
            upd = jnp.concatenate([top[:, 0:h2], top[:, h2:t] + bot], axis=1)
        acc_s[n] = alpha * acc_s[n] + upd
        m_s[n] = m_new

    def step(j_next, nxt, j_cur, cur):
        for n in range(n_chains):
            if nxt is not None:
                scores(n, j_next, *nxt)
            if cur is not None:
                consume(n, j_cur, *cur)

    buf_a, buf_b = (s_a, mx_a), (s_b, mx_b)
    step(0, buf_a, None, None)

    def pair(pp, carry):
        j = 2 * pp
        step(j + 1, buf_b, j, buf_a)
        step(j + 2, buf_a, j + 1, buf_b)
        return carry

    lax.fori_loop(0, i // 2, pair, 0)

    lam = (jnp.exp(jnp.sum(lq1_ref[...] * lk1_ref[...], axis=-1, keepdims=True))
           - jnp.exp(jnp.sum(lq2_ref[...] * lk2_ref[...], axis=-1, keepdims=True))
           + lambda_init)
    dv = DIFF_VDIM

    def last_tile(buf):
        for hh in range(DIFF_HPS):
            a0, a1 = 2 * hh, 2 * hh + 1
            consume(a0, i, *buf, diagonal=True)
            consume(a1, i, *buf, diagonal=True)
            ot = (acc_s[a0, 0:dv, :] / acc_s[a0, dv:dv + 1, :]
                  - lam * (acc_s[a1, 0:dv, :] / acc_s[a1, dv:dv + 1, :]))
            o = ot.T
            ms = jnp.mean(o * o, axis=-1, keepdims=True)
            o = o * lax.rsqrt(ms + EPS) * subln_ref[...] * (1.0 - lambda_init)
            o_ref[0, :, hh * dv:(hh + 1) * dv] = o.astype(BF16)

    @pl.when(i % 2 == 0)
    def _():
        last_tile(buf_a)

    @pl.when(i % 2 == 1)
    def _():
        step(i, buf_b, i - 1, buf_a)
        last_tile(buf_b)


def _diffattn(qt, k, vt, lq1, lk1, lq2, lk2, subln, lambda_init):
    B, S, _ = k.shape
    t = ATTN_T
    n = DIFF_HPS
    hw = 2 * DIFF_QK
    nch = 2 * n
    vec = pl.BlockSpec((1, DIFF_QK), lambda b, h, i: (0, 0))
    return pl.pallas_call(
        functools.partial(_diff_kernel, lambda_init=lambda_init),
        grid=(B, DIFF_HEADS // n, S // t),
        in_specs=[pl.BlockSpec((1, 1, n * hw, t), lambda b, h, i: (b, i, h, 0)),
                  pl.BlockSpec((1, S, n * hw), lambda b, h, i: (b, 0, h)),
                  pl.BlockSpec((1, S // t, n * DIFF_VDIM, t), lambda b, h, i: (b, 0, h, 0)),
                  vec, vec, vec, vec,
                  pl.BlockSpec((1, DIFF_VDIM), lambda b, h, i: (0, 0))],
        out_specs=pl.BlockSpec((1, t, n * DIFF_VDIM), lambda b, h, i: (b, i, h)),
        out_shape=jax.ShapeDtypeStruct((B, S, DIFF_WIDTH), BF16),
        scratch_shapes=[pltpu.VMEM((n, 2, t, t), F32),
                        pltpu.VMEM((nch, 1, t), F32),
                        pltpu.VMEM((nch, DIFF_VDIM + DIFF_ONES_ROWS, t), F32),
                        pltpu.VMEM((nch, t, t), F32),
                        pltpu.VMEM((nch, t, t), F32),
                        pltpu.VMEM((nch, 1, t), F32),
                        pltpu.VMEM((nch, 1, t), F32)],
        compiler_params=_cparams(("parallel", "arbitrary", "arbitrary")),
        name="diffattn",
    )(qt, k, vt, lq1, lk1, lq2, lk2, subln)


def _memkv_kernel(mem_ref, g_ref, wk_ref, wv_ref, kn_ref, k_ref, v_ref):
    x = mem_ref[0]
    ms = jnp.mean(x * x, axis=-1, keepdims=True)
    xn = (x * lax.rsqrt(ms + EPS) * g_ref[...]).astype(BF16)
    k = _dot(xn, wk_ref[...])
    v = _dot(xn, wv_ref[...])
    ks = []
    for h in range(XA_HEADS):
        kh = k[:, h * XA_HEAD:(h + 1) * XA_HEAD]
        kms = jnp.mean(kh * kh, axis=-1, keepdims=True)
        ks.append(kh * lax.rsqrt(kms + EPS) * kn_ref[...])
    k_ref[0] = jnp.concatenate(ks, axis=1).astype(BF16)
    v_ref[0] = v.astype(BF16)


def _memkv(mem, g, wk_bf, wv_bf, k_norm):
    B, M, D = mem.shape
    const = lambda shape: pl.BlockSpec(shape, lambda b: (0,) * len(shape))
    tile = pl.BlockSpec((1, M, D), lambda b: (b, 0, 0))
    return pl.pallas_call(
        _memkv_kernel,
        grid=(B,),
        in_specs=[tile, const((1, D)), const((D, D)), const((D, D)), const((1, XA_HEAD))],
        out_specs=[tile, tile],
        out_shape=[jax.ShapeDtypeStruct((B, M, D), BF16), jax.ShapeDtypeStruct((B, M, D), BF16)],
        compiler_params=_cparams(("parallel",)),
        name="memkv",
    )(mem, g, wk_bf, wv_bf, k_norm)


def _mid_kernel(h_ref, yp_ref, yr_ref, yd_ref, wout_ref, g_ref, wq_ref, km_ref, vm_ref,
                wo_ref, qn_ref, o_ref):
    a = POOL_WIDTH
    b = POOL_WIDTH + RWKV_WIDTH
    h1 = (h_ref[0] + _dot(yp_ref[0], wout_ref[0:a, :]) + _dot(yr_ref[0], wout_ref[a:b, :])
          + _dot(yd_ref[0], wout_ref[b:D_MODEL, :]))
    ms = jnp.mean(h1 * h1, axis=-1, keepdims=True)
    xn = (h1 * lax.rsqrt(ms + EPS) * g_ref[...]).astype(BF16)
    q = _dot(xn, wq_ref[...])
    qscale = (XA_HEAD ** -0.5) * LOG2E
    heads = [slice(hd * XA_HEAD, (hd + 1) * XA_HEAD) for hd in range(XA_HEADS)]
    scores = []
    for sl in heads:
        qh = q[:, sl]
        qms = jnp.mean(qh * qh, axis=-1, keepdims=True)
        qh = (qh * lax.rsqrt(qms + EPS) * (qn_ref[...] * qscale)).astype(BF16)
        scores.append(_dot_nt(qh, km_ref[0, :, sl]))
    probs, sums = [], []
    for s in scores:
        p = jnp.exp2(s - jnp.max(s, axis=-1, keepdims=True))
        sums.append(jnp.sum(p, axis=-1, keepdims=True))
        probs.append(p.astype(BF16))
    outs = [_dot(p, vm_ref[0, :, sl]) / l for p, l, sl in zip(probs, sums, heads)]
    o = jnp.concatenate(outs, axis=1).astype(BF16)
    for n in range(0, D_MODEL, OUT_BLOCK):
        cols = slice(n, n + OUT_BLOCK)
        o_ref[0, :, cols] = h1[:, cols] + _dot(o, wo_ref[:, cols])


def _mid(h, yp, yr, yd, wout_bf, g, wq_bf, kmem, vmem, wo_bf, q_norm, tm):
    B, S, D = h.shape
    M = kmem.shape[1]
    const = lambda shape: pl.BlockSpec(shape, lambda b, i: (0,) * len(shape),
                                       pipeline_mode=pl.Buffered(1))
    tile = lambda c: pl.BlockSpec((1, tm, c), lambda b, i: (b, i, 0))
    memspec = pl.BlockSpec((1, M, D), lambda b, i: (b, 0, 0))
    return pl.pallas_call(
        _mid_kernel,
        grid=(B, S // tm),
        in_specs=[tile(D), tile(POOL_WIDTH), tile(RWKV_WIDTH), tile(DIFF_WIDTH), const((D, D)),
                  const((1, D)), const((D, D)), memspec, memspec, const((D, D)),
                  const((1, XA_HEAD))],
        out_specs=tile(D),
        out_shape=jax.ShapeDtypeStruct((B, S, D), F32),
        compiler_params=_cparams(("parallel", "parallel")),
        name="mid",
    )(h, yp, yr, yd, wout_bf, g, wq_bf, kmem, vmem, wo_bf, q_norm)


def _ffn_kernel(h_ref, g_ref, wup_ref, cw_ref, cb_ref, wd_ref, o_ref, abuf, carry, *, tm, nf):
    i = pl.program_id(1)
    pad = V7X_SUBLANES
    tf = D_FF // nf
    x = h_ref[0]
    ms = jnp.mean(x * x, axis=-1, keepdims=True)
    xn = (x * lax.rsqrt(ms + EPS) * g_ref[...]).astype(BF16)
    for j in range(nf):
        fc = slice(j * tf, (j + 1) * tf)
        a = _dot(xn, wup_ref[:, fc])
        b = _dot(xn, wup_ref[:, D_FF + j * tf:D_FF + (j + 1) * tf])
        abuf[0:pad, :] = jnp.where(i == 0, 0.0, carry[j])
        abuf[pad:pad + tm, :] = a
        carry[j] = a[tm - pad:tm, :]
        cw = cw_ref[:, fc]
        c = (cw[2:3, :] * a + cw[1:2, :] * abuf[pad - 1:pad - 1 + tm, :]
             + cw[0:1, :] * abuf[pad - 2:pad - 2 + tm, :] + cb_ref[:, fc])
        gelu = 0.5 * c * (1.0 + lax.erf(c * (2.0 ** -0.5)))
        hmid = (gelu * b).astype(BF16)
        for n in range(0, D_MODEL, OUT_BLOCK):
            cols = slice(n, n + OUT_BLOCK)
            base = x[:, cols] if j == 0 else o_ref[0, :, cols]
            o_ref[0, :, cols] = base + _dot(hmid, wd_ref[fc, cols])


def _ffn(h, g, wup_bf, conv_w, conv_b, wdown_bf, tm, nf):
    B, S, D = h.shape
    tf = D_FF // nf
    once = pl.Buffered(1)
    return pl.pallas_call(
        functools.partial(_ffn_kernel, tm=tm, nf=nf),
        grid=(B, S // tm),
        in_specs=[pl.BlockSpec((1, tm, D), lambda b, i: (b, i, 0)),
                  pl.BlockSpec((1, D), lambda b, i: (0, 0)),
                  pl.BlockSpec((D, 2 * D_FF), lambda b, i: (0, 0), pipeline_mode=once),
                  pl.BlockSpec((CONV_W, D_FF), lambda b, i: (0, 0)),
                  pl.BlockSpec((1, D_FF), lambda b, i: (0, 0)),
                  pl.BlockSpec((D_FF, D), lambda b, i: (0, 0), pipeline_mode=once)],
        out_specs=pl.BlockSpec((1, tm, D), lambda b, i: (b, i, 0)),
        out_shape=jax.ShapeDtypeStruct((B, S, D), F32),
        scratch_shapes=[pltpu.VMEM((V7X_SUBLANES + tm, tf), F32),
                        pltpu.VMEM((nf, V7X_SUBLANES, tf), F32)],
        compiler_params=_cparams(("parallel", "arbitrary")),
        name="ffn",
    )(h, g, wup_bf, conv_w, conv_b, wdown_bf)


def _tiles(S):
    pick = lambda pref: max(c for c in (64, 128, 256, 512, 1024) if c <= pref and S % c == 0)
    assert S % ATTN_T == 0
    return dict(mix=pick(1024), rwkv=pick(512), mid=pick(1024),
                ffn=pick(1024))


def _block_diag(blocks):
    n = len(blocks)
    rows = []
    for i, blk in enumerate(blocks):
        rows.append(jnp.concatenate(
            [blk if j == i else jnp.zeros_like(blk) for j in range(n)], axis=1))
    return jnp.concatenate(rows, axis=0)


def kernel(x, mem, mix_norm_g, w_in, pool_w, pool_scale, rwkv_mu, rwkv_w0, rwkv_w2, rwkv_a0, rwkv_a2, rwkv_g2, rwkv_k_k, rwkv_k_a, rwkv_r_k, rwkv_ln_w, rwkv_ln_b, diff_q_norm, diff_k_norm, diff_lq1, diff_lk1, diff_lq2, diff_lk2, diff_subln, w_out, xa_norm_g, mem_norm_g, xa_wq, xa_wk, xa_wv, xa_wo, xa_q_norm, xa_k_norm, ffn_norm_g, ffn_w_up, ffn_conv_w, ffn_conv_b, ffn_w_down):
    B, S, D = x.shape
    depth = w_in.shape[0]
    tl = _tiles(S)
    row = lambda a: a.reshape(1, -1).astype(F32)
    h = x
    for l in range(depth):
        lambda_init = 0.8 - 0.6 * math.exp(-0.3 * l)
        qgain = row(jnp.tile(diff_q_norm[l].reshape(-1), DIFF_HEADS)) * (DIFF_QK ** -0.5 * LOG2E)
        kgain = row(jnp.tile(diff_k_norm[l].reshape(-1), DIFF_HEADS))
        zeros64 = jnp.zeros((64, RWKV_WIDTH), F32)
        w2p = jnp.concatenate([rwkv_w2[l], zeros64], axis=0).astype(BF16)
        a2p = jnp.concatenate([zeros64, rwkv_a2[l]], axis=0).astype(BF16)
        pool_bd = _block_diag([pool_w[l, gi] for gi in range(len(POOL_WINDOWS))]).astype(BF16)

        y_pool, z_rwkv, qd, kd, vd = _mix_in(h, row(mix_norm_g[l]), w_in[l].astype(BF16),
                                             qgain, kgain, pool_bd, row(pool_scale[l]), tl["mix"])
        y_rwkv = _rwkv(z_rwkv, row(rwkv_mu[l]), row(rwkv_w0[l]), w2p, row(rwkv_a0[l]), a2p,
                       rwkv_g2[l].astype(BF16), row(rwkv_k_k[l]), row(rwkv_k_a[l]),
                       row(rwkv_r_k[l]), row(rwkv_ln_w[l]), row(rwkv_ln_b[l]), tl["rwkv"])
        y_diff = _diffattn(qd, kd, vd, row(diff_lq1[l]), row(diff_lk1[l]), row(diff_lq2[l]),
                           row(diff_lk2[l]), row(diff_subln[l]), lambda_init)
        kmem, vmem = _memkv(mem, row(mem_norm_g[l]), xa_wk[l].astype(BF16),
                            xa_wv[l].astype(BF16), row(xa_k_norm[l]))
        h = _mid(h, y_pool, y_rwkv, y_diff, w_out[l].astype(BF16), row(xa_norm_g[l]),
                 xa_wq[l].astype(BF16), kmem, vmem, xa_wo[l].astype(BF16), row(xa_q_norm[l]),
                 tl["mid"])
        h = _ffn(h, row(ffn_norm_g[l]), ffn_w_up[l].astype(BF16), ffn_conv_w[l].astype(F32),
                 row(ffn_conv_b[l]), ffn_w_down[l].astype(BF16), tl["ffn"], 2)
    return h
```

```python
import functools
import math

import jax
import jax.numpy as jnp
import numpy as np
from jax import lax
from jax.experimental import pallas as pl
from jax.experimental.pallas import tpu as pltpu

F32 = jnp.float32
BF16 = jnp.bfloat16

D_MODEL = 1024
EPS = 1e-6
CHUNK = 64

POOL_WIDTH = 256
POOL_GDIM = 64
POOL_WINDOWS = (2, 4, 8, 16)
POOL_HALO = 16

RWKV_WIDTH = 256
RWKV_HEAD = 64
RWKV_COLS = 1024
RWKV_GN_EPS = 64e-5
RWKV_CHUNK = 64

DIFF_WIDTH = 512
DIFF_HEADS = 4
DIFF_VDIM = 128
DIFF_QK = 64
P_IN = POOL_WIDTH + RWKV_COLS + 3 * DIFF_WIDTH

XA_HEADS = 4
XA_HEAD = 256
D_FF = 2816
CONV_W = 3
OUT_BLOCK = 256

LOG2E = math.log2(math.e)
NEG_BIG = -1e30
ATTN_T = 512

V7X_SUBLANES = 8
V7X_VMEM_LIMIT = 52 * 1024 * 1024


def _cparams(sem):
    return pltpu.CompilerParams(dimension_semantics=sem, vmem_limit_bytes=V7X_VMEM_LIMIT)


def _dot(a, b):
    return jnp.dot(a, b, preferred_element_type=F32)


def _dot_nt(a, b):
    return lax.dot_general(a, b, (((1,), (1,)), ((), ())), preferred_element_type=F32)


def _dot_tn(a, b):
    return lax.dot_general(a, b, (((0,), (0,)), ((), ())), preferred_element_type=F32)


def _group_ones(width, group):
    idx = np.arange(width) // group
    return jnp.asarray((idx[:, None] == idx[None, :]).astype(np.float32), dtype=BF16)


def _gsum1(x, ones_bd):
    w = ones_bd.shape[0]
    parts = [_dot(x[:, i:i + w].astype(BF16), ones_bd) for i in range(0, x.shape[1], w)]
    return parts[0] if len(parts) == 1 else jnp.concatenate(parts, axis=1)


def _gsum2(x, ones_bd):
    hi = x.astype(BF16)
    lo = (x - hi.astype(F32)).astype(BF16)
    return _dot(hi, ones_bd) + _dot(lo, ones_bd)


def _pool_mixer(u, halo, buf_ref, t0, w_bd, scale):
    tm = u.shape[0]
    pad = V7X_SUBLANES
    n = tm + POOL_HALO
    buf_ref[0:pad, :] = jnp.zeros((pad, POOL_WIDTH), F32)
    buf_ref[pad:pad + POOL_HALO, :] = halo
    buf_ref[pad + POOL_HALO:pad + n, :] = u
    lane = lax.broadcasted_iota(jnp.int32, (tm, POOL_WIDTH), 1)
    grp = lane // POOL_GDIM
    win = jnp.zeros((tm, POOL_WIDTH), F32)
    shift = 1
    for gi, w in enumerate(POOL_WINDOWS):
        while shift < w:
            cur = buf_ref[pad:pad + n, :] + buf_ref[pad - shift:pad - shift + n, :]
            buf_ref[pad:pad + n, :] = cur
            shift *= 2
        win = jnp.where(grp == gi, buf_ref[pad + POOL_HALO:pad + n, :], win)
    t = t0 + lax.broadcasted_iota(jnp.int32, (tm, POOL_WIDTH), 0)
    wlane = jnp.left_shift(2, grp)
    count = jnp.minimum(t + 1, wlane).astype(F32)
    d = win / count - u
    return _dot(d.astype(BF16), w_bd) * scale


def _mix_in_kernel(x_ref, g_ref, w_ref, ones_ref, qg_ref, kg_ref, pw_ref, ps_ref,
                   yp_ref, zr_ref, q_ref, k_ref, v_ref, pbuf, phalo, *, tm):
    i = pl.program_id(1)
    x = x_ref[0]
    ms = jnp.mean(x * x, axis=-1, keepdims=True)
    xn = (x * lax.rsqrt(ms + EPS) * g_ref[...]).astype(BF16)
    z = _dot(xn, w_ref[...])
    u = z[:, :POOL_WIDTH]
    halo = jnp.where(i == 0, 0.0, phalo[...])
    phalo[...] = u[tm - POOL_HALO:tm, :]
    yp_ref[0] = _pool_mixer(u, halo, pbuf, i * tm, pw_ref[...], ps_ref[...]).astype(BF16)
    zr_ref[0] = z[:, POOL_WIDTH:POOL_WIDTH + RWKV_COLS]
    o = POOL_WIDTH + RWKV_COLS
    q = z[:, o:o + DIFF_WIDTH]
    k = z[:, o + DIFF_WIDTH:o + 2 * DIFF_WIDTH]
    v = z[:, o + 2 * DIFF_WIDTH:o + 3 * DIFF_WIDTH]
    ones_bd = ones_ref[...]
    qss = _gsum1(q * q, ones_bd) * (1.0 / DIFF_QK)
    kss = _gsum1(k * k, ones_bd) * (1.0 / DIFF_QK)
    qn = q * lax.rsqrt(qss + EPS) * qg_ref[...]
    k_ref[0] = (k * lax.rsqrt(kss + EPS) * kg_ref[...]).astype(BF16)
    for n in range(tm // ATTN_T):
        rows = slice(n * ATTN_T, (n + 1) * ATTN_T)
        q_ref[0, n] = qn[rows, :].T.astype(BF16)
        v_ref[0, n] = v[rows, :].T.astype(BF16)


def _mix_in(h, g, w_in_bf, qgain, kgain, pool_bd_bf, pool_scale, tm):
    B, S, D = h.shape
    ones_bd = _group_ones(256, DIFF_QK)
    const = lambda shape: pl.BlockSpec(shape, lambda b, i: (0,) * len(shape),
                                       pipeline_mode=pl.Buffered(1))
    tile = lambda c: pl.BlockSpec((1, tm, c), lambda b, i: (b, i, 0))
    tile_t = pl.BlockSpec((1, tm // ATTN_T, DIFF_WIDTH, ATTN_T), lambda b, i: (b, i, 0, 0))
    shape_t = jax.ShapeDtypeStruct((B, S // ATTN_T, DIFF_WIDTH, ATTN_T), BF16)
    return pl.pallas_call(
        functools.partial(_mix_in_kernel, tm=tm),
        grid=(B, S // tm),
        in_specs=[tile(D), const((1, D)), const((D, P_IN)), const((256, 256)),
                  const((1, DIFF_WIDTH)), const((1, DIFF_WIDTH)),
                  const((POOL_WIDTH, POOL_WIDTH)), const((1, POOL_WIDTH))],
        out_specs=[tile(POOL_WIDTH), tile(RWKV_COLS), tile_t, tile(DIFF_WIDTH), tile_t],
        out_shape=[jax.ShapeDtypeStruct((B, S, POOL_WIDTH), BF16),
                   jax.ShapeDtypeStruct((B, S, RWKV_COLS), F32),
                   shape_t,
                   jax.ShapeDtypeStruct((B, S, DIFF_WIDTH), BF16),
                   shape_t],
        scratch_shapes=[pltpu.VMEM((V7X_SUBLANES + POOL_HALO + tm, POOL_WIDTH), F32),
                        pltpu.VMEM((POOL_HALO, POOL_WIDTH), F32)],
        compiler_params=_cparams(("parallel", "arbitrary")),
        name="mix_in",
    )(h, g, w_in_bf, ones_bd, qgain, kgain, pool_bd_bf, pool_scale)


RWKV_PAIR = 2 * RWKV_HEAD


def _bd2(x, lo_mask):
    x = x.astype(BF16)
    zero = jnp.zeros_like(x)
    return jnp.concatenate([jnp.where(lo_mask, x, zero), jnp.where(lo_mask, zero, x)], axis=0)


def _diag_blocks(full, lo_mask):
    n = full.shape[1] // RWKV_PAIR
    lo = jnp.concatenate([lo_mask] * n, axis=1) if n > 1 else lo_mask
    return jnp.where(lo, full[0:RWKV_HEAD], full[RWKV_HEAD:2 * RWKV_HEAD])


def _rwkv_kernel(z_ref, halo_ref, mu_ref, w0_ref, w2_ref, a0_ref, a2_ref, g2_ref,
                 kk_ref, ka_ref, rk_ref, lnw_ref, lnb_ref, ones_ref, tri_ref,
                 y_ref,
                 zbuf, r_s, k_s, v_s, lw_s, a_s, b_s, y_s, state, rh_s, g_s, ml_s, gate_s, lw3_s,
                 *, tm):
    i = pl.program_id(1)
    L = RWKV_CHUNK
    W = RWKV_WIDTH
    pad = V7X_SUBLANES

    @pl.when(i == 0)
    def _():
        state[...] = jnp.zeros_like(state)

    zbuf[0:pad, :] = jnp.where(i == 0, 0.0, halo_ref[0])
    zbuf[pad:pad + tm, :] = z_ref[0]
    ones_bd = ones_ref[...]

    def tokenwise(r0, r1):
        z = zbuf[pad + r0:pad + r1, :]
        zprev = zbuf[pad - 1 + r0:pad - 1 + r1, :]
        zm = z + mu_ref[...] * (zprev - z)
        yield
        r = zm[:, 0:W]
        k = zm[:, W:2 * W]
        z6 = zm[:, 3 * W:3 * W + 128]
        gd = zm[:, 3 * W + 128:3 * W + 256]
        r_s[r0:r1, :] = r
        v_s[r0:r1, :] = zm[:, 2 * W:3 * W]
        wl = w0_ref[...] + _dot(jnp.tanh(z6).astype(BF16), w2_ref[...])
        w = -jax.nn.softplus(-wl) - 0.5
        yield
        a = jax.nn.sigmoid(a0_ref[...] + _dot(z6.astype(BF16), a2_ref[...]))
        gate_s[r0:r1, :] = _dot(jax.nn.sigmoid(gd).astype(BF16), g2_ref[...])
        yield
        kk = k * kk_ref[...]
        kk = kk * lax.rsqrt(jnp.maximum(_gsum1(kk * kk, ones_bd), 1e-24))
        k_s[r0:r1, :] = k * (1.0 + (a - 1.0) * ka_ref[...])
        a_s[r0:r1, :] = -kk
        b_s[r0:r1, :] = kk * a
        yield
        rest = -jnp.exp(w)
        lw_s[r0:r1, :] = rest
        for n in range(3):
            term = rest.astype(BF16)
            lw3_s[n, r0:r1, :] = term
            rest = rest - term.astype(F32)
        yield

    PW = RWKV_PAIR
    n_pairs = W // PW
    lane2 = lax.broadcasted_iota(jnp.int32, (L, PW), 1)
    trow = lax.broadcasted_iota(jnp.int32, (L, PW), 0)
    lo_mask = lane2 < RWKV_HEAD
    jcol = jnp.bitwise_and(lane2, RWKV_HEAD - 1)
    strict = trow > jcol
    incl = trow >= jcol
    eye = trow == jcol
    bf = lambda x: x.astype(BF16)
    rows2 = lambda x, y: jnp.concatenate([x, y], axis=0)

    def local_stages(chunk_ids):
        chains = [(ci, pg) for ci in chunk_ids for pg in range(n_pairs)]
        pre = {}
        for ci in chunk_ids:
            sl = slice(ci * L, (ci + 1) * L)
            lw = lw_s[sl, :]
            c_in = (_dot(tri_ref[...], lw3_s[0, sl, :]) + _dot(tri_ref[...], lw3_s[1, sl, :])
                    + _dot(tri_ref[...], lw3_s[2, sl, :]))
            c_tot = c_in[L - 1:L, :]
            e_neg = jnp.exp(-c_in)
            e_rem = jnp.exp(c_tot - c_in)
            pre[ci] = dict(
                at=a_s[sl, :] * jnp.exp(c_in - lw), rt=r_s[sl, :] * jnp.exp(c_in),
                bt=b_s[sl, :] * e_neg, kt=k_s[sl, :] * e_neg,
                bbar=b_s[sl, :] * e_rem, kbar=k_s[sl, :] * e_rem,
                v=v_s[sl, :], e_tot=jnp.exp(c_tot))
        pair = lambda ci, pg, name: pre[ci][name][:, pg * PW:(pg + 1) * PW]
        yield
        st = []
        for ci, pg in chains:
            at, rt = pair(ci, pg, "at"), pair(ci, pg, "rt")
            lhs = bf(rows2(at, rt))
            tb = _dot_nt(lhs, _bd2(pair(ci, pg, "bt"), lo_mask))
            tk = _dot_nt(lhs, _bd2(pair(ci, pg, "kt"), lo_mask))
            tab = jnp.where(strict, tb[0:L], 0.0)
            st.append(dict(
                at=at, rt=rt, v=pair(ci, pg, "v"), tab=tab,
                trb=bf(jnp.where(incl, tb[L:2 * L], 0.0)),
                tak=jnp.where(strict, tk[0:L], 0.0), trk=jnp.where(incl, tk[L:2 * L], 0.0),
                winv=jnp.where(eye, 1.0, 0.0) + tab))
        yield
        for c in st:
            c["p"] = _dot(bf(c["tab"]), _bd2(c["tab"], lo_mask))
        yield
        for _ in range(int(math.log2(L)) - 2):
            for c in st:
                res = _dot(bf(rows2(c["p"], c["winv"])), _bd2(c["p"], lo_mask))
                c["p"] = res[0:L]
                c["winv"] = c["winv"] + res[L:2 * L]
            yield
        for c in st:
            c["winv"] = c["winv"] + _dot(bf(c["winv"]), _bd2(c["p"], lo_mask))
            c["tv"] = _dot(bf(rows2(c["tak"], c["trk"])), _bd2(c["v"], lo_mask))
        yield
        for c in st:
            x_bd = jnp.concatenate([_bd2(c["at"], lo_mask), _bd2(c["tv"][0:L], lo_mask)], axis=1)
            c["wx"] = _dot(bf(c["winv"]), x_bd)
        yield
        for c in st:
            wx = c["wx"]
            ax_bd = jnp.concatenate([_bd2(wx[:, 0:PW], lo_mask), _bd2(wx[:, PW:2 * PW], lo_mask)],
                                    axis=1)
            zz = _dot(c["trb"], ax_bd)
            c["rh"] = c["rt"] + zz[:, 0:PW]
            c["yl"] = zz[:, PW:2 * PW] + c["tv"][L:2 * L]
        yield "last stage next"
        for n, ((ci, pg), c) in enumerate(zip(chains, st)):
            if n and n % n_pairs == 0:
                yield
            v = c["v"]
            lhs_t = bf(rows2(pair(ci, pg, "bbar"), pair(ci, pg, "kbar")))
            rhs_t = bf(rows2(c["wx"], jnp.concatenate([jnp.zeros_like(v), v], axis=1)))
            full = _diag_blocks(_dot_tn(lhs_t, rhs_t), lo_mask)
            gs = slice(pg * PW, (pg + 1) * PW)
            sl = slice(ci * L, (ci + 1) * L)
            rh_s[sl, gs] = bf(c["rh"])
            y_s[sl, gs] = c["yl"]
            g_s[ci, :, gs] = bf(full[:, 0:PW] + jnp.where(eye, pre[ci]["e_tot"][:, gs], 0.0))
            ml_s[ci, :, gs] = full[:, PW:2 * PW]
        yield

    m_cur = [state[:, g * PW:(g + 1) * PW] for g in range(n_pairs)]

    def state_steps(chunk_ids):
        for ci in chunk_ids:
            sl = slice(ci * L, (ci + 1) * L)
            for pg in range(n_pairs):
                gs = slice(pg * PW, (pg + 1) * PW)
                res = _dot(rows2(rh_s[sl, gs], g_s[ci, :, gs]), _bd2(m_cur[pg], lo_mask))
                y_s[sl, gs] = y_s[sl, gs] + res[0:L]
                m_cur[pg] = res[L:2 * L] + ml_s[ci, :, gs]
            yield

    def emit(*gens):
        live = list(gens)
        while live:
            for gen in list(live):
                if next(gen, "done") == "done":
                    live.remove(gen)

    chunks = list(range(tm // L))
    emit(tokenwise(0, tm))
    stages = local_stages(chunks)
    for tag in stages:
        if tag == "last stage next":
            break
    emit(stages, state_steps(chunks))
    for pg in range(n_pairs):
        state[:, pg * PW:(pg + 1) * PW] = m_cur[pg]

    y = y_s[...]
    r = r_s[...]
    kp = k_s[...]
    v = v_s[...]
    inv_n = 1.0 / RWKV_HEAD
    mean = _gsum2(y, ones_bd) * inv_n
    yc = y - mean
    var = _gsum1(yc * yc, ones_bd) * inv_n
    yn = yc * lax.rsqrt(var + RWKV_GN_EPS) * lnw_ref[...] + lnb_ref[...]
    bonus = _gsum1(r * kp * rk_ref[...], ones_bd) * v
    y_ref[0] = ((yn + bonus) * gate_s[...]).astype(BF16)


def _rwkv(z_rwkv, mu, w0, w2p, a0, a2p, g2, k_k, k_a, r_k, ln_w, ln_b, tm):
    B, S, C = z_rwkv.shape
    W = RWKV_WIDTH
    L = RWKV_CHUNK
    assert tm % (2 * L) == 0
    ones_bd = _group_ones(W, RWKV_HEAD)
    tri = jnp.asarray((np.arange(L)[:, None] >= np.arange(L)[None, :]).astype(np.float32),
                      dtype=BF16)
    r = tm // V7X_SUBLANES
    const = lambda shape: pl.BlockSpec(shape, lambda b, i: (0,) * len(shape))
    vec = const((1, W))
    sq = const((W, W))
    return pl.pallas_call(
        functools.partial(_rwkv_kernel, tm=tm),
        grid=(B, S // tm),
        in_specs=[pl.BlockSpec((1, tm, C), lambda b, i: (b, i, 0)),
                  pl.BlockSpec((1, V7X_SUBLANES, C),
                               lambda b, i: (b, jnp.maximum(i * r - 1, 0), 0)),
                  const((1, C)), vec, const((128, W)), vec, const((128, W)), const((128, W)),
                  vec, vec, vec, vec, vec, sq, const((L, L))],
        out_specs=pl.BlockSpec((1, tm, W), lambda b, i: (b, i, 0)),
        out_shape=jax.ShapeDtypeStruct((B, S, W), BF16),
        scratch_shapes=[pltpu.VMEM((V7X_SUBLANES + tm, C), F32)]
        + [pltpu.VMEM((tm, W), F32) for _ in range(7)]
        + [pltpu.VMEM((RWKV_HEAD, W), F32),
           pltpu.VMEM((tm, W), BF16),
           pltpu.VMEM((tm // L, RWKV_HEAD, W), BF16),
           pltpu.VMEM((tm // L, RWKV_HEAD, W), F32),
           pltpu.VMEM((tm, W), F32),
           pltpu.VMEM((3, tm, W), BF16)],
        compiler_params=_cparams(("parallel", "arbitrary")),
        name="rwkv",
    )(z_rwkv, z_rwkv, mu, w0, w2p, a0, a2p, g2, k_k, k_a, r_k, ln_w, ln_b, ones_bd, tri)


DIFF_ONES_ROWS = 16
DIFF_HPS = 4


def _diff_kernel(qt_ref, k_ref, vt_ref, lq1_ref, lk1_ref, lq2_ref, lk2_ref, subln_ref,
                 o_ref, bias2, m_s, acc_s, s_a, s_b, mx_a, mx_b, *, lambda_init):
    t = ATTN_T
    hw = 2 * DIFF_QK
    hp = pl.program_id(1)
    i = pl.program_id(2)
    slopes = []
    for hh in range(DIFF_HPS):
        s2 = jnp.float32(0.0)
        for hd in range(DIFF_HEADS):
            s2 = jnp.where(hp * DIFF_HPS + hh == hd,
                           2.0 ** (-8.0 * (hd + 1) / DIFF_HEADS) * LOG2E, s2)
        slopes.append(s2)

    @pl.when(i == 0)
    def _():
        kc = lax.broadcasted_iota(jnp.int32, (t, t), 0)
        qr = lax.broadcasted_iota(jnp.int32, (t, t), 1)
        vis = (kc // CHUNK) <= (qr // CHUNK)
        rel = (qr - jnp.abs(qr - kc)).astype(F32)
        for hh in range(DIFF_HPS):
            bias2[hh, 0] = slopes[hh] * kc.astype(F32)
            bias2[hh, 1] = jnp.where(vis, slopes[hh] * rel, NEG_BIG)

    row = lax.broadcasted_iota(jnp.int32, (hw, t), 0)
    qc = []
    for hh in range(DIFF_HPS):
        qt = qt_ref[0, 0, hh * hw:(hh + 1) * hw, :]
        qc.append(jnp.where(row < DIFF_QK, qt, jnp.zeros_like(qt)))
        qc.append(jnp.where(row >= DIFF_QK, qt, jnp.zeros_like(qt)))
    ones = jnp.ones((DIFF_ONES_ROWS, t), BF16)

    m_s[...] = jnp.full(m_s.shape, NEG_BIG, F32)
    acc_s[...] = jnp.zeros_like(acc_s)

    n_chains = 2 * DIFF_HPS

    def scores(n, j, buf, mx):
        hh = n // 2
        sl = pl.ds(pl.multiple_of(j * t, t), t)
        kt = k_ref[0, sl, hh * hw:(hh + 1) * hw]
        sb = _dot(kt, qc[n]) + bias2[hh, (j == i).astype(jnp.int32)]
        buf[n] = sb
        mx[n] = jnp.max(sb, axis=0, keepdims=True)

    def consume(n, j, buf, mx, diagonal=False):
        hh = n // 2
        cj = slopes[hh] * ((j - i) * t).astype(F32)
        vaug = jnp.concatenate([vt_ref[0, j, hh * DIFF_VDIM:(hh + 1) * DIFF_VDIM, :], ones],
                               axis=0)
        m_old = m_s[n]
        m_new = jnp.maximum(m_old, mx[n] + cj)
        alpha = jnp.exp2(m_old - m_new)
        shift = m_new - cj
        if not diagonal:
            upd = _dot(vaug, jnp.exp2(buf[n] - shift).astype(BF16))
        else:
            h2 = t // 2
            p_top = jnp.exp2(buf[n, 0:h2, :] - shift)
            p_bot = jnp.exp2(buf[n, h2:t, h2:t] - shift[:, h2:t])
            top = _dot(vaug[:, 0:h2], p_top.astype(BF16))
            bot = _dot(vaug[:, h2:t], p_bot.astype(BF16))
            upd = jnp.concatenate([top[:, 0:h2], top[:, h2:t] + bot], axis=1)
        acc_s[n] = alpha * acc_s[n] + upd
        m_s[n] = m_new

    def step(j_next, nxt, j_cur, cur):
        for n in range(n_chains):
            if nxt is not None:
                scores(n, j_next, *nxt)
            if cur is not None:
                consume(n, j_cur, *cur)

    buf_a, buf_b = (s_a, mx_a), (s_b, mx_b)
    step(0, buf_a, None, None)

    def pair(pp, carry):
        j = 2 * pp
        step(j + 1, buf_b, j, buf_a)
        step(j + 2, buf_a, j + 1, buf_b)
        return carry

    lax.fori_loop(0, i // 2, pair, 0)

    lam = (jnp.exp(jnp.sum(lq1_ref[...] * lk1_ref[...], axis=-1, keepdims=True))
           - jnp.exp(jnp.sum(lq2_ref[...] * lk2_ref[...], axis=-1, keepdims=True))
           + lambda_init)
    dv = DIFF_VDIM

    def last_tile(buf):
        for hh in range(DIFF_HPS):
            a0, a1 = 2 * hh, 2 * hh + 1
            consume(a0, i, *buf, diagonal=True)
            consume(a1, i, *buf, diagonal=True)
            ot = (acc_s[a0, 0:dv, :] / acc_s[a0, dv:dv + 1, :]
                  - lam * (acc_s[a1, 0:dv, :] / acc_s[a1, dv:dv + 1, :]))
            o = ot.T
            ms = jnp.mean(o * o, axis=-1, keepdims=True)
            o = o * lax.rsqrt(ms + EPS) * subln_ref[...] * (1.0 - lambda_init)
            o_ref[0, :, hh * dv:(hh + 1) * dv] = o.astype(BF16)

    @pl.when(i % 2 == 0)
    def _():
        last_tile(buf_a)

    @pl.when(i % 2 == 1)
    def _():
        step(i, buf_b, i - 1, buf_a)
        last_tile(buf_b)


def _diffattn(qt, k, vt, lq1, lk1, lq2, lk2, subln, lambda_init):
    B, S, _ = k.shape
    t = ATTN_T
    n = DIFF_HPS
    hw = 2 * DIFF_QK
    nch = 2 * n
    vec = pl.BlockSpec((1, DIFF_QK), lambda b, h, i: (0, 0))
    return pl.pallas_call(
        functools.partial(_diff_kernel, lambda_init=lambda_init),
        grid=(B, DIFF_HEADS // n, S // t),
        in_specs=[pl.BlockSpec((1, 1, n * hw, t), lambda b, h, i: (b, i, h, 0)),
                  pl.BlockSpec((1, S, n * hw), lambda b, h, i: (b, 0, h),
                               pipeline_mode=pl.Buffered(1)),
                  pl.BlockSpec((1, S // t, n * DIFF_VDIM, t), lambda b, h, i: (b, 0, h, 0),
                               pipeline_mode=pl.Buffered(1)),
                  vec, vec, vec, vec,
                  pl.BlockSpec((1, DIFF_VDIM), lambda b, h, i: (0, 0))],
        out_specs=pl.BlockSpec((1, t, n * DIFF_VDIM), lambda b, h, i: (b, i, h)),
        out_shape=jax.ShapeDtypeStruct((B, S, DIFF_WIDTH), BF16),
        scratch_shapes=[pltpu.VMEM((n, 2, t, t), F32),
                        pltpu.VMEM((nch, 1, t), F32),
                        pltpu.VMEM((nch, DIFF_VDIM + DIFF_ONES_ROWS, t), F32),
                        pltpu.VMEM((nch, t, t), F32),
                        pltpu.VMEM((nch, t, t), F32),
                        pltpu.VMEM((nch, 1, t), F32),
                        pltpu.VMEM((nch, 1, t), F32)],
        compiler_params=_cparams(("parallel", "arbitrary", "arbitrary")),
        name="diffattn",
    )(qt, k, vt, lq1, lk1, lq2, lk2, subln)


def _memkv_kernel(mem_ref, g_ref, wk_ref, wv_ref, kn_ref, k_ref, v_ref):
    x = mem_ref[0]
    ms = jnp.mean(x * x, axis=-1, keepdims=True)
    xn = (x * lax.rsqrt(ms + EPS) * g_ref[...]).astype(BF16)
    k = _dot(xn, wk_ref[...])
    v = _dot(xn, wv_ref[...])
    ks = []
    for h in range(XA_HEADS):
        kh = k[:, h * XA_HEAD:(h + 1) * XA_HEAD]
        kms = jnp.mean(kh * kh, axis=-1, keepdims=True)
        ks.append(kh * lax.rsqrt(kms + EPS) * kn_ref[...])
    k_ref[0] = jnp.concatenate(ks, axis=1).astype(BF16)
    v_ref[0] = v.astype(BF16)


def _memkv(mem, g, wk_bf, wv_bf, k_norm):
    B, M, D = mem.shape
    const = lambda shape: pl.BlockSpec(shape, lambda b: (0,) * len(shape))
    tile = pl.BlockSpec((1, M, D), lambda b: (b, 0, 0))
    return pl.pallas_call(
        _memkv_kernel,
        grid=(B,),
        in_specs=[tile, const((1, D)), const((D, D)), const((D, D)), const((1, XA_HEAD))],
        out_specs=[tile, tile],
        out_shape=[jax.ShapeDtypeStruct((B, M, D), BF16), jax.ShapeDtypeStruct((B, M, D), BF16)],
        compiler_params=_cparams(("parallel",)),
        name="memkv",
    )(mem, g, wk_bf, wv_bf, k_norm)


def _mid_kernel(h_ref, yp_ref, yr_ref, yd_ref, wout_ref, g_ref, wq_ref, km_ref, vm_ref,
                wo_ref, qn_ref, o_ref):
    a = POOL_WIDTH
    b = POOL_WIDTH + RWKV_WIDTH
    h1 = (h_ref[0] + _dot(yp_ref[0], wout_ref[0:a, :]) + _dot(yr_ref[0], wout_ref[a:b, :])
          + _dot(yd_ref[0], wout_ref[b:D_MODEL, :]))
    ms = jnp.mean(h1 * h1, axis=-1, keepdims=True)
    xn = (h1 * lax.rsqrt(ms + EPS) * g_ref[...]).astype(BF16)
    q = _dot(xn, wq_ref[...])
    qscale = (XA_HEAD ** -0.5) * LOG2E
    heads = [slice(hd * XA_HEAD, (hd + 1) * XA_HEAD) for hd in range(XA_HEADS)]
    scores = []
    for sl in heads:
        qh = q[:, sl]
        qms = jnp.mean(qh * qh, axis=-1, keepdims=True)
        qh = (qh * lax.rsqrt(qms + EPS) * (qn_ref[...] * qscale)).astype(BF16)
        scores.append(_dot_nt(qh, km_ref[0, :, sl]))
    probs, sums = [], []
    for s in scores:
        p = jnp.exp2(s - jnp.max(s, axis=-1, keepdims=True))
        sums.append(jnp.sum(p, axis=-1, keepdims=True))
        probs.append(p.astype(BF16))
    outs = [_dot(p, vm_ref[0, :, sl]) / l for p, l, sl in zip(probs, sums, heads)]
    o = jnp.concatenate(outs, axis=1).astype(BF16)
    for n in range(0, D_MODEL, OUT_BLOCK):
        cols = slice(n, n + OUT_BLOCK)
        o_ref[0, :, cols] = h1[:, cols] + _dot(o, wo_ref[:, cols])


def _mid(h, yp, yr, yd, wout_bf, g, wq_bf, kmem, vmem, wo_bf, q_norm, tm):
    B, S, D = h.shape
    M = kmem.shape[1]
    const = lambda shape: pl.BlockSpec(shape, lambda b, i: (0,) * len(shape),
                                       pipeline_mode=pl.Buffered(1))
    tile = lambda c: pl.BlockSpec((1, tm, c), lambda b, i: (b, i, 0))
    memspec = pl.BlockSpec((1, M, D), lambda b, i: (b, 0, 0))
    return pl.pallas_call(
        _mid_kernel,
        grid=(B, S // tm),
        in_specs=[tile(D), tile(POOL_WIDTH), tile(RWKV_WIDTH), tile(DIFF_WIDTH), const((D, D)),
                  const((1, D)), const((D, D)), memspec, memspec, const((D, D)),
                  const((1, XA_HEAD))],
        out_specs=tile(D),
        out_shape=jax.ShapeDtypeStruct((B, S, D), F32),
        compiler_params=_cparams(("parallel", "parallel")),
        name="mid",
    )(h, yp, yr, yd, wout_bf, g, wq_bf, kmem, vmem, wo_bf, q_norm)


def _ffn_kernel(h_ref, g_ref, wup_ref, cw_ref, cb_ref, wd_ref, o_ref, abuf, carry, *, tm, nf):
    i = pl.program_id(1)
    pad = V7X_SUBLANES
    tf = D_FF // nf
    x = h_ref[0]
    ms = jnp.mean(x * x, axis=-1, keepdims=True)
    xn = (x * lax.rsqrt(ms + EPS) * g_ref[...]).astype(BF16)
    for j in range(nf):
        fc = slice(j * tf, (j + 1) * tf)
        a = _dot(xn, wup_ref[:, fc])
        b = _dot(xn, wup_ref[:, D_FF + j * tf:D_FF + (j + 1) * tf])
        abuf[0:pad, :] = jnp.where(i == 0, 0.0, carry[j])
        abuf[pad:pad + tm, :] = a
        carry[j] = a[tm - pad:tm, :]
        cw = cw_ref[:, fc]
        c = (cw[2:3, :] * a + cw[1:2, :] * abuf[pad - 1:pad - 1 + tm, :]
             + cw[0:1, :] * abuf[pad - 2:pad - 2 + tm, :] + cb_ref[:, fc])
        gelu = 0.5 * c * (1.0 + lax.erf(c * (2.0 ** -0.5)))
        hmid = (gelu * b).astype(BF16)
        for n in range(0, D_MODEL, OUT_BLOCK):
            cols = slice(n, n + OUT_BLOCK)
            base = x[:, cols] if j == 0 else o_ref[0, :, cols]
            o_ref[0, :, cols] = base + _dot(hmid, wd_ref[fc, cols])


def _ffn(h, g, wup_bf, conv_w, conv_b, wdown_bf, tm, nf):
    B, S, D = h.shape
    tf = D_FF // nf
    once = pl.Buffered(1)
    return pl.pallas_call(
        functools.partial(_ffn_kernel, tm=tm, nf=nf),
        grid=(B, S // tm),
        in_specs=[pl.BlockSpec((1, tm, D), lambda b, i: (b, i, 0)),
                  pl.BlockSpec((1, D), lambda b, i: (0, 0)),
                  pl.BlockSpec((D, 2 * D_FF), lambda b, i: (0, 0), pipeline_mode=once),
                  pl.BlockSpec((CONV_W, D_FF), lambda b, i: (0, 0)),
                  pl.BlockSpec((1, D_FF), lambda b, i: (0, 0)),
                  pl.BlockSpec((D_FF, D), lambda b, i: (0, 0), pipeline_mode=once)],
        out_specs=pl.BlockSpec((1, tm, D), lambda b, i: (b, i, 0)),
        out_shape=jax.ShapeDtypeStruct((B, S, D), F32),
        scratch_shapes=[pltpu.VMEM((V7X_SUBLANES + tm, tf), F32),
                        pltpu.VMEM((nf, V7X_SUBLANES, tf), F32)],
        compiler_params=_cparams(("parallel", "arbitrary")),
        name="ffn",
    )(h, g, wup_bf, conv_w, conv_b, wdown_bf)


def _tiles(S):
    pick = lambda pref: max(c for c in (64, 128, 256, 512, 1024) if c <= pref and S % c == 0)
    assert S % ATTN_T == 0
    return dict(mix=pick(1024), rwkv=pick(512), mid=pick(1024),
                ffn=pick(1024))


def _block_diag(blocks):
    n = len(blocks)
    rows = []
    for i, blk in enumerate(blocks):
        rows.append(jnp.concatenate(
            [blk if j == i else jnp.zeros_like(blk) for j in range(n)], axis=1))
    return jnp.concatenate(rows, axis=0)


def kernel(x, mem, mix_norm_g, w_in, pool_w, pool_scale, rwkv_mu, rwkv_w0, rwkv_w2, rwkv_a0, rwkv_a2, rwkv_g2, rwkv_k_k, rwkv_k_a, rwkv_r_k, rwkv_ln_w, rwkv_ln_b, diff_q_norm, diff_k_norm, diff_lq1, diff_lk1, diff_lq2, diff_lk2, diff_subln, w_out, xa_norm_g, mem_norm_g, xa_wq, xa_wk, xa_wv, xa_wo, xa_q_norm, xa_k_norm, ffn_norm_g, ffn_w_up, ffn_conv_w, ffn_conv_b, ffn_w_down):
    B, S, D = x.shape
    depth = w_in.shape[0]
    tl = _tiles(S)
    row = lambda a: a.reshape(1, -1).astype(F32)
    h = x
    for l in range(depth):
        lambda_init = 0.8 - 0.6 * math.exp(-0.3 * l)
        qgain = row(jnp.tile(diff_q_norm[l].reshape(-1), DIFF_HEADS)) * (DIFF_QK ** -0.5 * LOG2E)
        kgain = row(jnp.tile(diff_k_norm[l].reshape(-1), DIFF_HEADS))
        zeros64 = jnp.zeros((64, RWKV_WIDTH), F32)
        w2p = jnp.concatenate([rwkv_w2[l], zeros64], axis=0).astype(BF16)
        a2p = jnp.concatenate([zeros64, rwkv_a2[l]], axis=0).astype(BF16)
        pool_bd = _block_diag([pool_w[l, gi] for gi in range(len(POOL_WINDOWS))]).astype(BF16)

        y_pool, z_rwkv, qd, kd, vd = _mix_in(h, row(mix_norm_g[l]), w_in[l].astype(BF16),
                                             qgain, kgain, pool_bd, row(pool_scale[l]), tl["mix"])
        y_rwkv = _rwkv(z_rwkv, row(rwkv_mu[l]), row(rwkv_w0[l]), w2p, row(rwkv_a0[l]), a2p,
                       rwkv_g2[l].astype(BF16), row(rwkv_k_k[l]), row(rwkv_k_a[l]),
                       row(rwkv_r_k[l]), row(rwkv_ln_w[l]), row(rwkv_ln_b[l]), tl["rwkv"])
        y_diff = _diffattn(qd, kd, vd, row(diff_lq1[l]), row(diff_lk1[l]), row(diff_lq2[l]),
                           row(diff_lk2[l]), row(diff_subln[l]), lambda_init)
        kmem, vmem = _memkv(mem, row(mem_norm_g[l]), xa_wk[l].astype(BF16),
                            xa_wv[l].astype(BF16), row(xa_k_norm[l]))
        h = _mid(h, y_pool, y_rwkv, y_diff, w_out[l].astype(BF16), row(xa_norm_g[l]),
                 xa_wq[l].astype(BF16), kmem, vmem, xa_wo[l].astype(BF16), row(xa_q_norm[l]),
                 tl["mid"])
        h = _ffn(h, row(ffn_norm_g[l]), ffn_w_up[l].astype(BF16), ffn_conv_w[l].astype(F32),
                 row(ffn_conv_b[l]), ffn_w_down[l].astype(BF16), tl["ffn"], 2)
    return h
```
